```python
import math
import jax, jax.numpy as jnp
from jax import lax
import numpy as np

D_MODEL = 1024
BATCH = 2
SEQ = 16384
DEPTH = 4

GRID_W = 64
CTX_LEN = 256
Q_BLOCK = 128
ROPE_THETA = 10000.0
EPS = 1e-6

MLA_HEADS = 4
MLA_Q_LORA = 256
MLA_KV_LORA = 128
MLA_NOPE = 64
MLA_ROPE = 32
MLA_V = 64
MLA_QK = MLA_NOPE + MLA_ROPE
MLA_IN = MLA_Q_LORA + MLA_KV_LORA + MLA_ROPE

GQA_Q_HEADS = 4
GQA_KV_HEADS = 2
GQA_GROUP = GQA_Q_HEADS // GQA_KV_HEADS
GQA_HEAD_DIM = 64
GQA_IN = (GQA_Q_HEADS + 2 * GQA_KV_HEADS) * GQA_HEAD_DIM

ML_HEADS = 4
ML_HEAD_DIM = 64
ML_WIDTH = ML_HEADS * ML_HEAD_DIM
ML_CHUNK = 64
ML_IN = 4 * ML_WIDTH + 4 * ML_HEADS

HY_WIDTH = 256
HY_ORDER = 2
HY_BANDS = 16
HY_FEAT = 1 + 2 * HY_BANDS
HY_HIDDEN = 64
HY_IN = (HY_ORDER + 1) * HY_WIDTH
HY_DECAY_TARGET = 1e-2
HY_SHORT_PCT = 0.3
HY_LONG_PCT = 1.5

N_BRANCH = 4
BRANCH_W = 256
GATE_IN = N_BRANCH * D_MODEL
IN_OFFSETS = (MLA_IN, MLA_IN + GQA_IN, MLA_IN + GQA_IN + ML_IN, MLA_IN + GQA_IN + ML_IN + HY_IN)
D_IN = MLA_IN + GQA_IN + ML_IN + HY_IN + GATE_IN

PEER_HEADS = 8
PEER_NKEYS = 128
PEER_EXPERTS = PEER_NKEYS * PEER_NKEYS
PEER_DK = 256
PEER_TOPK = 16
PEER_BLOCK = 128

kernel_name = "hybrid_prefix_dit_mla_gqa_mlstm_hyena_peer"


def rms_norm(x, g):
    xf = x.astype(jnp.float32)
    y = xf * lax.rsqrt(jnp.mean(xf * xf, axis=-1, keepdims=True) + EPS)
    return (y * g.astype(jnp.float32)).astype(x.dtype)


def modulate(x, g, shift, scale):
    return rms_norm(x, g) * (1.0 + scale) + shift


def axial_rope_tables(rows, cols, d_rot):
    m = d_rot // 2
    inv = ROPE_THETA ** (-jnp.arange(0, m, 2, dtype=jnp.float32) / m)
    ar = rows.astype(jnp.float32)[:, None] * inv
    ac = cols.astype(jnp.float32)[:, None] * inv
    return (jnp.cos(ar), jnp.sin(ar), jnp.cos(ac), jnp.sin(ac))


def _rotate(x, cos, sin):
    x1, x2 = jnp.split(x, 2, axis=-1)
    return jnp.concatenate([x1 * cos - x2 * sin, x2 * cos + x1 * sin], axis=-1)


def apply_axial_rope(x, tables):
    extra = x.ndim - 3
    t = [a.reshape((a.shape[0],) + (1,) * extra + (a.shape[1],)) for a in tables]
    xr, xc = jnp.split(x.astype(jnp.float32), 2, axis=-1)
    out = jnp.concatenate([_rotate(xr, t[0], t[1]), _rotate(xc, t[2], t[3])], axis=-1)
    return out.astype(x.dtype)


def attend(q, k, v, scale):
    s = jnp.einsum('bqhgd,bkhd->bhgqk', q, k, preferred_element_type=jnp.float32) * scale
    p = jax.nn.softmax(s, axis=-1).astype(v.dtype)
    return jnp.einsum('bhgqk,bkhd->bqhgd', p, v)


def latent_attention(q_lat, k_ctx, v_ctx, k_lat, v_lat, scale):
    k_all = jnp.concatenate([k_ctx, k_lat], axis=1)
    v_all = jnp.concatenate([v_ctx, v_lat], axis=1)
    B, S = q_lat.shape[:2]
    qb = jnp.moveaxis(q_lat.reshape((B, S // Q_BLOCK, Q_BLOCK) + q_lat.shape[2:]), 1, 0)
    ob = lax.map(lambda qi: attend(qi, k_all, v_all, scale), qb)
    return jnp.moveaxis(ob, 0, 1).reshape(B, S, -1)


def mla_qkv(u, q_norm, kv_norm, w_uq, w_ukv, qk_norm_q, qk_norm_k, rope):
    B, L = u.shape[:2]
    c_q = u[..., :MLA_Q_LORA]
    c_kv = u[..., MLA_Q_LORA:MLA_Q_LORA + MLA_KV_LORA]
    k_pe = u[..., MLA_Q_LORA + MLA_KV_LORA:]
    q = (rms_norm(c_q, q_norm) @ w_uq).reshape(B, L, MLA_HEADS, MLA_QK)
    kv = (rms_norm(c_kv, kv_norm) @ w_ukv).reshape(B, L, MLA_HEADS, MLA_NOPE + MLA_V)
    k = jnp.concatenate([kv[..., :MLA_NOPE],
                         jnp.broadcast_to(k_pe[:, :, None, :], (B, L, MLA_HEADS, MLA_ROPE))], axis=-1)
    v = kv[..., MLA_NOPE:]
    q = rms_norm(q, qk_norm_q)
    k = rms_norm(k, qk_norm_k)
    if rope is not None:
        q = jnp.concatenate([q[..., :MLA_NOPE], apply_axial_rope(q[..., MLA_NOPE:], rope)], axis=-1)
        k = jnp.concatenate([k[..., :MLA_NOPE], apply_axial_rope(k[..., MLA_NOPE:], rope)], axis=-1)
    return q[:, :, :, None, :], k, v


def gqa_qkv(u, qk_norm_q, qk_norm_k, rope):
    B, L = u.shape[:2]
    nq = GQA_Q_HEADS * GQA_HEAD_DIM
    nk = GQA_KV_HEADS * GQA_HEAD_DIM
    q = rms_norm(u[..., :nq].reshape(B, L, GQA_KV_HEADS, GQA_GROUP, GQA_HEAD_DIM), qk_norm_q)
    k = rms_norm(u[..., nq:nq + nk].reshape(B, L, GQA_KV_HEADS, GQA_HEAD_DIM), qk_norm_k)
    v = u[..., nq + nk:].reshape(B, L, GQA_KV_HEADS, GQA_HEAD_DIM)
    if rope is not None:
        q = apply_axial_rope(q, rope)
        k = apply_axial_rope(k, rope)
    return q, k, v


def mlstm_scan(q, k, v, log_i, log_f, state):
    B, L, H, d = q.shape
    nc = L // ML_CHUNK
    f32 = jnp.float32

    def to_chunks(a):
        return jnp.moveaxis(a.reshape((B, nc, ML_CHUNK) + a.shape[2:]), 1, 0)

    xs = (to_chunks(q.astype(f32)), to_chunks(k.astype(f32) * (d ** -0.5)), to_chunks(v.astype(f32)),
          to_chunks(log_i), to_chunks(log_f))
    lower = jnp.tril(jnp.ones((ML_CHUNK, ML_CHUNK), dtype=bool))

    def step(carry, inp):
        C, n, m = carry
        qc, kc, vc, ic, fc = inp
        b = jnp.cumsum(fc, axis=1).transpose(0, 2, 1)
        ih = ic.transpose(0, 2, 1)
        dmat = jnp.where(lower, b[..., :, None] - b[..., None, :] + ih[..., None, :], -jnp.inf)
        inter = b + m[..., None]
        m_t = jnp.maximum(inter, jnp.max(dmat, axis=-1))
        w_intra = jnp.exp(dmat - m_t[..., None])
        w_state = jnp.exp(inter - m_t)
        a = w_intra * jnp.einsum('bthd,bshd->bhts', qc, kc)
        num = (jnp.einsum('bhts,bshd->bthd', a, vc)
               + jnp.einsum('bht,bthk,bhkv->bthv', w_state, qc, C))
        den = jnp.sum(a, axis=-1) + w_state * jnp.einsum('bthk,bhk->bht', qc, n)
        h = num / jnp.maximum(jnp.abs(den), jnp.exp(-m_t)).transpose(0, 2, 1)[..., None]
        b_last = b[..., -1]
        g = b_last[..., None] - b + ih
        m_new = jnp.maximum(b_last + m, jnp.max(g, axis=-1))
        w_s = jnp.exp(g - m_new[..., None])
        w_c = jnp.exp(b_last + m - m_new)
        C = w_c[..., None, None] * C + jnp.einsum('bhs,bshk,bshv->bhkv', w_s, kc, vc)
        n = w_c[..., None] * n + jnp.einsum('bhs,bshk->bhk', w_s, kc)
        return (C, n, m_new), h

    state, hs = lax.scan(step, state, xs)
    return jnp.moveaxis(hs, 0, 1).reshape(B, L, H, d), state


def mlstm_inputs(u, gate_b):
    B, L = u.shape[:2]
    q, k, v, o = [a.reshape(B, L, ML_HEADS, ML_HEAD_DIM) for a in jnp.split(u[..., :4 * ML_WIDTH], 4, axis=-1)]
    gates = u[..., 4 * ML_WIDTH:].astype(jnp.float32).reshape(B, L, 4, ML_HEADS) + gate_b.astype(jnp.float32)
    return (q, k, v, o, gates[:, :, 0], gates[:, :, 1],
            jax.nn.log_sigmoid(gates[:, :, 2]), jax.nn.log_sigmoid(gates[:, :, 3]))


def bidir_mlstm(q, k, v, i_f, i_b, lf_f, lf_b, st_f, st_b):
    h_f, st_f = mlstm_scan(q, k, v, i_f, lf_f, st_f)
    flip = lambda a: jnp.flip(a, axis=1)
    h_b, st_b = mlstm_scan(flip(q), flip(k), flip(v), flip(i_b), flip(lf_b), st_b)
    return h_f + flip(h_b), st_f, st_b


def mlstm_mixer(u_ctx, u_lat, gate_b, norm_g, with_ctx):
    B = u_lat.shape[0]
    f32 = jnp.float32
    zero = (jnp.zeros((B, ML_HEADS, ML_HEAD_DIM, ML_HEAD_DIM), f32),
            jnp.zeros((B, ML_HEADS, ML_HEAD_DIM), f32), jnp.zeros((B, ML_HEADS), f32))
    qc, kc, vc, oc, icf, icb, fcf, fcb = mlstm_inputs(u_ctx, gate_b)
    h_c, st_f, st_b = bidir_mlstm(qc, kc, vc, icf, icb, fcf, fcb, zero, zero)
    ql, kl, vl, ol, ilf, ilb, flf, flb = mlstm_inputs(u_lat, gate_b)
    h_l, _, _ = bidir_mlstm(ql, kl, vl, ilf, ilb, flf, flb, st_f, st_b)

    def out(h, o):
        Bq, L = o.shape[:2]
        y = jax.nn.sigmoid(o.astype(f32)) * rms_norm(h, norm_g)
        return y.reshape(Bq, L, ML_WIDTH).astype(o.dtype)

    return (out(h_c, oc) if with_ctx else None), out(h_l, ol)


def hyena_filters(L, w1, b1, freq, w2, b2, w3):
    f32 = jnp.float32
    t = jnp.arange(L, dtype=f32) / L
    ang = 2.0 * math.pi * t[:, None] * jnp.arange(1, HY_BANDS + 1, dtype=f32)
    z = jnp.concatenate([t[:, None], jnp.sin(ang), jnp.cos(ang)], axis=-1)
    hdn = jnp.sin(freq[0].astype(f32) * (z @ w1.astype(f32) + b1.astype(f32)))
    hdn = jnp.sin(freq[1].astype(f32) * (hdn @ w2.astype(f32) + b2.astype(f32)))
    filt = (hdn @ w3.astype(f32)).reshape(L, HY_ORDER, 2, HY_WIDTH)
    log_target = math.log(HY_DECAY_TARGET)
    alpha = jnp.linspace(-log_target / HY_LONG_PCT, -log_target / HY_SHORT_PCT, HY_WIDTH, dtype=f32)
    filt = filt * jnp.exp(-t[:, None] * alpha)[:, None, None, :]
    return filt * lax.rsqrt(jnp.sum(filt * filt, axis=(0, 2), keepdims=True) + EPS)


def two_sided_fftconv(z, h_fwd, h_bwd):
    L, C = h_fwd.shape
    h_circ = jnp.concatenate([h_fwd, jnp.zeros((1, C), h_fwd.dtype), h_bwd[:0:-1]], axis=0)
    hf = jnp.fft.rfft(h_circ, n=2 * L, axis=0)
    zf = jnp.fft.rfft(z.astype(jnp.float32), n=2 * L, axis=1)
    y = jnp.fft.irfft(zf * hf[None], n=2 * L, axis=1)[:, :L]
    return y.astype(z.dtype)


def short_conv3(u, w):
    up = jnp.pad(u, ((0, 0), (1, 1), (0, 0)))
    return up[:, :-2] * w[0] + up[:, 1:-1] * w[1] + up[:, 2:] * w[2]


def hyena_mixer(u, conv_w, filt, bias):
    u = short_conv3(u, conv_w)
    v, x1, x2 = jnp.split(u, 3, axis=-1)
    z = v
    for order, gate in enumerate((x1, x2)):
        z = gate * (two_sided_fftconv(z, filt[:, order, 0], filt[:, order, 1]) + z * bias[order])
    return z


def merge_branches(ys, gate_logits, w_branch, w_out):
    B, L = gate_logits.shape[:2]
    gl = gate_logits.reshape(B, L, N_BRANCH, -1)
    acc = jax.nn.sigmoid(gl[:, :, 0]) * (ys[0] @ w_branch[0])
    for i in range(1, N_BRANCH):
        acc = acc + jax.nn.sigmoid(gl[:, :, i]) * (ys[i] @ w_branch[i])
    return acc @ w_out


def peer(h, wq, subkeys, u_tab, v_tab):
    T, D = h.shape
    q = (h @ wq).reshape(T, PEER_HEADS, 2, PEER_DK // 2)
    s = jnp.einsum('thcd,hcnd->thcn', q, subkeys, preferred_element_type=jnp.float32)
    sv, si = lax.top_k(s, PEER_TOPK)
    cand = (sv[:, :, 0, :, None] + sv[:, :, 1, None, :]).reshape(T, PEER_HEADS, PEER_TOPK * PEER_TOPK)
    cidx = (si[:, :, 0, :, None] * PEER_NKEYS + si[:, :, 1, None, :]).reshape(T, PEER_HEADS, PEER_TOPK * PEER_TOPK)
    best, pos = lax.top_k(cand, PEER_TOPK)
    idx = jnp.take_along_axis(cidx, pos, axis=-1)
    g = jax.nn.softmax(best, axis=-1)
    nb = T // PEER_BLOCK

    def block(args):
        hb, ib, gb = args
        act = jax.nn.gelu(jnp.einsum('thkd,td->thk', u_tab[ib], hb, preferred_element_type=jnp.float32))
        return jnp.einsum('thk,thkd->td', (gb * act).astype(v_tab.dtype), v_tab[ib])

    out = lax.map(block, (h.reshape(nb, PEER_BLOCK, D),
                          idx.reshape(nb, PEER_BLOCK, PEER_HEADS, PEER_TOPK),
                          g.reshape(nb, PEER_BLOCK, PEER_HEADS, PEER_TOPK)))
    return out.reshape(T, D).astype(h.dtype)


def trunk_layer(xc, xl, mod_c, mod_l, rope_a, rope_b, with_ctx, norm1, norm2, w_in,
                mla_q_norm, mla_kv_norm, mla_w_uq, mla_w_ukv, mla_qk_norm_q, mla_qk_norm_k,
                gqa_qk_norm_q, gqa_qk_norm_k, ml_gate_b, ml_norm,
                hy_conv, hy_w1, hy_b1, hy_freq, hy_w2, hy_b2, hy_w3, hy_bias,
                w_branch, w_out, peer_wq, peer_subkeys, peer_u, peer_v):
    D = xl.shape[-1]
    sh1_c, sc1_c, g1_c, sh2_c, sc2_c, g2_c = jnp.split(mod_c, 6, axis=-1)
    sh1_l, sc1_l, g1_l, sh2_l, sc2_l, g2_l = [a[:, None, :] for a in jnp.split(mod_l, 6, axis=-1)]

    pc = modulate(xc, norm1, sh1_c, sc1_c) @ w_in
    pl = modulate(xl, norm1, sh1_l, sc1_l) @ w_in
    a_c, b_c, m_c, h_c, gate_c = jnp.split(pc, IN_OFFSETS, axis=-1)
    a_l, b_l, m_l, h_l, gate_l = jnp.split(pl, IN_OFFSETS, axis=-1)

    qa_c, ka_c, va_c = mla_qkv(a_c, mla_q_norm, mla_kv_norm, mla_w_uq, mla_w_ukv, mla_qk_norm_q, mla_qk_norm_k, None)
    qa_l, ka_l, va_l = mla_qkv(a_l, mla_q_norm, mla_kv_norm, mla_w_uq, mla_w_ukv, mla_qk_norm_q, mla_qk_norm_k, rope_a)
    ya_l = latent_attention(qa_l, ka_c, va_c, ka_l, va_l, MLA_QK ** -0.5)
    qb_c, kb_c, vb_c = gqa_qkv(b_c, gqa_qk_norm_q, gqa_qk_norm_k, None)
    qb_l, kb_l, vb_l = gqa_qkv(b_l, gqa_qk_norm_q, gqa_qk_norm_k, rope_b)
    yb_l = latent_attention(qb_l, kb_c, vb_c, kb_l, vb_l, GQA_HEAD_DIM ** -0.5)
    yc_c, yc_l = mlstm_mixer(m_c, m_l, ml_gate_b, ml_norm, with_ctx)
    filt_l = hyena_filters(xl.shape[1], hy_w1, hy_b1, hy_freq, hy_w2, hy_b2, hy_w3)
    yd_l = hyena_mixer(h_l, hy_conv, filt_l, hy_bias)

    xl = xl + g1_l * merge_branches([ya_l, yb_l, yc_l, yd_l], gate_l, w_branch, w_out)
    if with_ctx:
        Bc, Lc = xc.shape[:2]
        ya_c = attend(qa_c, ka_c, va_c, MLA_QK ** -0.5).reshape(Bc, Lc, -1)
        yb_c = attend(qb_c, kb_c, vb_c, GQA_HEAD_DIM ** -0.5).reshape(Bc, Lc, -1)
        filt_c = hyena_filters(Lc, hy_w1, hy_b1, hy_freq, hy_w2, hy_b2, hy_w3)
        yd_c = hyena_mixer(h_c, hy_conv, filt_c, hy_bias)
        xc = xc + g1_c * merge_branches([ya_c, yb_c, yc_c, yd_c], gate_c, w_branch, w_out)

    h2l = modulate(xl, norm2, sh2_l, sc2_l)
    if with_ctx:
        h2c = modulate(xc, norm2, sh2_c, sc2_c)
        n_ctx = xc.shape[0] * xc.shape[1]
        f = peer(jnp.concatenate([h2c.reshape(-1, D), h2l.reshape(-1, D)], axis=0), peer_wq, peer_subkeys, peer_u, peer_v)
        xc = xc + g2_c * f[:n_ctx].reshape(xc.shape)
        xl = xl + g2_l * f[n_ctx:].reshape(xl.shape)
    else:
        xl = xl + g2_l * peer(h2l.reshape(-1, D), peer_wq, peer_subkeys, peer_u, peer_v).reshape(xl.shape)
    return xc, xl


def setup_inputs(seed: int = 0) -> dict:
    key = jax.random.key(seed)
    keys = iter(jax.random.split(key, 48))
    f32 = jnp.float32

    def nrm(shape, scale):
        return jax.random.normal(next(keys), shape, f32) * scale

    def gain(shape):
        return 1.0 + nrm(shape, 0.02)

    L, D = DEPTH, D_MODEL
    forget_bias = jnp.linspace(3.0, 6.0, ML_HEADS, dtype=f32)
    return {
        "x": nrm((BATCH, SEQ, D), 1.0),
        "c": nrm((BATCH, D), 1.0),
        "ctx": nrm((BATCH, CTX_LEN, D), 1.0),
        "c_ctx": nrm((D,), 1.0),
        "ada_w": nrm((L, D, 6 * D), 0.5 * D ** -0.5),
        "ada_b": nrm((L, 6 * D), 0.02),
        "norm1": gain((L, D)),
        "norm2": gain((L, D)),
        "w_in": nrm((L, D, D_IN), D ** -0.5),
        "mla_q_norm": gain((L, MLA_Q_LORA)),
        "mla_kv_norm": gain((L, MLA_KV_LORA)),
        "mla_w_uq": nrm((L, MLA_Q_LORA, MLA_HEADS * MLA_QK), MLA_Q_LORA ** -0.5),
        "mla_w_ukv": nrm((L, MLA_KV_LORA, MLA_HEADS * (MLA_NOPE + MLA_V)), MLA_KV_LORA ** -0.5),
        "mla_qk_norm_q": gain((L, MLA_QK)),
        "mla_qk_norm_k": gain((L, MLA_QK)),
        "gqa_qk_norm_q": gain((L, GQA_HEAD_DIM)),
        "gqa_qk_norm_k": gain((L, GQA_HEAD_DIM)),
        "ml_gate_b": jnp.concatenate([nrm((L, 2, ML_HEADS), 0.1), forget_bias + nrm((L, 2, ML_HEADS), 0.1)], axis=1),
        "ml_norm": gain((L, ML_HEADS, ML_HEAD_DIM)),
        "hy_conv": nrm((L, 3, HY_IN), 3 ** -0.5),
        "hy_w1": nrm((L, HY_FEAT, HY_HIDDEN), HY_FEAT ** -0.5),
        "hy_b1": nrm((L, HY_HIDDEN), 0.1),
        "hy_freq": gain((L, 2, HY_HIDDEN)),
        "hy_w2": nrm((L, HY_HIDDEN, HY_HIDDEN), HY_HIDDEN ** -0.5),
        "hy_b2": nrm((L, HY_HIDDEN), 0.1),
        "hy_w3": nrm((L, HY_HIDDEN, HY_ORDER * 2 * HY_WIDTH), HY_HIDDEN ** -0.5),
        "hy_bias": nrm((L, HY_ORDER, HY_WIDTH), 0.5),
        "w_branch": nrm((L, N_BRANCH, BRANCH_W, D), BRANCH_W ** -0.5),
        "w_out": nrm((L, D, D), D ** -0.5),
        "peer_wq": nrm((L, D, PEER_HEADS * PEER_DK), D ** -0.5),
        "peer_subkeys": nrm((L, PEER_HEADS, 2, PEER_NKEYS, PEER_DK // 2), (PEER_DK // 2) ** -0.5),
        "peer_u": nrm((L, PEER_EXPERTS, D), D ** -0.5),
        "peer_v": nrm((L, PEER_EXPERTS, D), PEER_HEADS ** -0.5),
    }


def reference(x, c, ctx, c_ctx, ada_w, ada_b, norm1, norm2, w_in,
              mla_q_norm, mla_kv_norm, mla_w_uq, mla_w_ukv, mla_qk_norm_q, mla_qk_norm_k,
              gqa_qk_norm_q, gqa_qk_norm_k, ml_gate_b, ml_norm,
              hy_conv, hy_w1, hy_b1, hy_freq, hy_w2, hy_b2, hy_w3, hy_bias,
              w_branch, w_out, peer_wq, peer_subkeys, peer_u, peer_v):
    B, S, D = x.shape
    ROWS = S // GRID_W
    rows = jnp.broadcast_to(jnp.arange(ROWS, dtype=jnp.int32)[:, None], (ROWS, GRID_W)).reshape(-1)
    cols = jnp.broadcast_to(jnp.arange(GRID_W, dtype=jnp.int32)[None, :], (ROWS, GRID_W)).reshape(-1)
    rope_a = axial_rope_tables(rows, cols, MLA_ROPE)
    rope_b = axial_rope_tables(rows, cols, GQA_HEAD_DIM)
    s_lat = jax.nn.silu(c)
    s_ctx = jax.nn.silu(c_ctx)
    xc, xl = ctx, x
    for l in range(DEPTH):
        mod_l = s_lat @ ada_w[l] + ada_b[l]
        mod_c = s_ctx @ ada_w[l] + ada_b[l]
        xc, xl = trunk_layer(
            xc, xl, mod_c, mod_l, rope_a, rope_b, l < DEPTH - 1, norm1[l], norm2[l], w_in[l],
            mla_q_norm[l], mla_kv_norm[l], mla_w_uq[l], mla_w_ukv[l], mla_qk_norm_q[l], mla_qk_norm_k[l],
            gqa_qk_norm_q[l], gqa_qk_norm_k[l], ml_gate_b[l], ml_norm[l],
            hy_conv[l], hy_w1[l], hy_b1[l], hy_freq[l], hy_w2[l], hy_b2[l], hy_w3[l], hy_bias[l],
            w_branch[l], w_out[l], peer_wq[l], peer_subkeys[l], peer_u[l], peer_v[l])
    return xl
```

```python
import functools
import math

import jax
import jax.numpy as jnp
from jax import lax
from jax.experimental import pallas as pl
from jax.experimental.pallas import tpu as pltpu

D_MODEL = 1024
DEPTH = 4
GRID_W = 64
ROPE_THETA = 10000.0
EPS = 1e-6

MLA_HEADS = 4
MLA_Q_LORA = 256
MLA_KV_LORA = 128
MLA_NOPE = 64
MLA_ROPE = 32
MLA_V = 64
MLA_QK = MLA_NOPE + MLA_ROPE
MLA_IN = MLA_Q_LORA + MLA_KV_LORA + MLA_ROPE

GQA_Q_HEADS = 4
GQA_KV_HEADS = 2
GQA_GROUP = GQA_Q_HEADS // GQA_KV_HEADS
GQA_HEAD_DIM = 64
GQA_IN = (GQA_Q_HEADS + 2 * GQA_KV_HEADS) * GQA_HEAD_DIM

ML_HEADS = 4
ML_HEAD_DIM = 64
ML_WIDTH = ML_HEADS * ML_HEAD_DIM
ML_CHUNK = 64
ML_IN = 4 * ML_WIDTH + 4 * ML_HEADS

HY_WIDTH = 256
HY_ORDER = 2
HY_BANDS = 16
HY_FEAT = 1 + 2 * HY_BANDS
HY_HIDDEN = 64
HY_IN = (HY_ORDER + 1) * HY_WIDTH
HY_DECAY_TARGET = 1e-2
HY_SHORT_PCT = 0.3
HY_LONG_PCT = 1.5

N_BRANCH = 4
BRANCH_W = 256
GATE_IN = N_BRANCH * D_MODEL
IN_OFFSETS = (MLA_IN, MLA_IN + GQA_IN, MLA_IN + GQA_IN + ML_IN, MLA_IN + GQA_IN + ML_IN + HY_IN)

PEER_HEADS = 8
PEER_NKEYS = 128
PEER_DK = 256
PEER_TOPK = 16
PEER_BLOCK = 128

V7X_LANES = 128
V7X_VMEM_BYTES = 64 * 1024 * 1024

ATT_TQ = 256
ATT_TK = 1280


def _flash_kernel(q_ref, kt_ref, v_ref, o_ref, *, tk, nk):
    q = q_ref[0, 0]
    tq = q.shape[0]
    dv = v_ref.shape[-1]

    def body(j, carry):
        m, l, acc = carry
        start = pl.multiple_of(j * tk, tk)
        kt = kt_ref[0, 0, :, pl.ds(start, tk)]
        s = jnp.dot(q, kt, preferred_element_type=jnp.float32)
        m_new = jnp.maximum(m, jnp.max(s, axis=-1, keepdims=True))
        alpha = jnp.exp(m - m_new)
        p = jnp.exp(s - m_new)
        l = alpha * l + jnp.sum(p, axis=-1, keepdims=True)
        vv = v_ref[0, 0, pl.ds(start, tk), :]
        acc = alpha * acc + jnp.dot(p.astype(vv.dtype), vv, preferred_element_type=jnp.float32)
        return m_new, l, acc

    m0 = jnp.full((tq, 1), -jnp.inf, jnp.float32)
    l0 = jnp.zeros((tq, 1), jnp.float32)
    acc0 = jnp.zeros((tq, dv), jnp.float32)
    _, l, acc = lax.fori_loop(0, nk, body, (m0, l0, acc0))
    o_ref[0, 0] = (acc / l).astype(o_ref.dtype)


def flash_attention(q, kt, v, group):
    B, H, S, d = q.shape
    K = kt.shape[-1]
    dv = v.shape[-1]
    assert S % ATT_TQ == 0 and K % ATT_TK == 0
    nk = K // ATT_TK
    return pl.pallas_call(
        functools.partial(_flash_kernel, tk=ATT_TK, nk=nk),
        grid=(B, H, S // ATT_TQ),
        in_specs=[
            pl.BlockSpec((1, 1, ATT_TQ, d), lambda b, h, i: (b, h, i, 0)),
            pl.BlockSpec((1, 1, d, K), lambda b, h, i: (b, h // group, 0, 0)),
            pl.BlockSpec((1, 1, K, dv), lambda b, h, i: (b, h // group, 0, 0)),
        ],
        out_specs=pl.BlockSpec((1, 1, ATT_TQ, dv), lambda b, h, i: (b, h, i, 0)),
        out_shape=jax.ShapeDtypeStruct((B, H, S, dv), jnp.float32),
        compiler_params=pltpu.CompilerParams(
            dimension_semantics=("arbitrary", "arbitrary", "arbitrary"),
            vmem_limit_bytes=V7X_VMEM_BYTES * 3 // 4),
        name="flash_attention",
    )(q, kt, v)


def latent_attention(q_lat, k_ctx, v_ctx, k_lat, v_lat, scale):
    B, S, Hk, G, dk = q_lat.shape
    k_all = jnp.concatenate([k_ctx, k_lat], axis=1)
    v_all = jnp.concatenate([v_ctx, v_lat], axis=1)
    q = (q_lat * scale).astype(jnp.bfloat16).reshape(B, S, Hk * G, dk).transpose(0, 2, 1, 3)
    kt = k_all.astype(jnp.bfloat16).transpose(0, 2, 3, 1)
    v = v_all.astype(jnp.bfloat16).transpose(0, 2, 1, 3)
    o = flash_attention(q, kt, v, G)
    return o.transpose(0, 2, 1, 3).reshape(B, S, -1)


def rms_norm(x, g):
    xf = x.astype(jnp.float32)
    y = xf * lax.rsqrt(jnp.mean(xf * xf, axis=-1, keepdims=True) + EPS)
    return (y * g.astype(jnp.float32)).astype(x.dtype)


def modulate(x, g, shift, scale):
    return rms_norm(x, g) * (1.0 + scale) + shift


def axial_rope_tables(rows, cols, d_rot):
    m = d_rot // 2
    inv = ROPE_THETA ** (-jnp.arange(0, m, 2, dtype=jnp.float32) / m)
    ar = rows.astype(jnp.float32)[:, None] * inv
    ac = cols.astype(jnp.float32)[:, None] * inv
    return (jnp.cos(ar), jnp.sin(ar), jnp.cos(ac), jnp.sin(ac))


def _rotate(x, cos, sin):
    x1, x2 = jnp.split(x, 2, axis=-1)
    return jnp.concatenate([x1 * cos - x2 * sin, x2 * cos + x1 * sin], axis=-1)


def apply_axial_rope(x, tables):
    extra = x.ndim - 3
    t = [a.reshape((a.shape[0],) + (1,) * extra + (a.shape[1],)) for a in tables]
    xr, xc = jnp.split(x.astype(jnp.float32), 2, axis=-1)
    out = jnp.concatenate([_rotate(xr, t[0], t[1]), _rotate(xc, t[2], t[3])], axis=-1)
    return out.astype(x.dtype)


def attend(q, k, v, scale):
    s = jnp.einsum('bqhgd,bkhd->bhgqk', q, k, preferred_element_type=jnp.float32) * scale
    p = jax.nn.softmax(s, axis=-1).astype(v.dtype)
    return jnp.einsum('bhgqk,bkhd->bqhgd', p, v)


def mla_qkv(u, q_norm, kv_norm, w_uq, w_ukv, qk_norm_q, qk_norm_k, rope):
    B, L = u.shape[:2]
    c_q = u[..., :MLA_Q_LORA]
    c_kv = u[..., MLA_Q_LORA:MLA_Q_LORA + MLA_KV_LORA]
    k_pe = u[..., MLA_Q_LORA + MLA_KV_LORA:]
    q = (rms_norm(c_q, q_norm) @ w_uq).reshape(B, L, MLA_HEADS, MLA_QK)
    kv = (rms_norm(c_kv, kv_norm) @ w_ukv).reshape(B, L, MLA_HEADS, MLA_NOPE + MLA_V)
    k = jnp.concatenate([kv[..., :MLA_NOPE],
                         jnp.broadcast_to(k_pe[:, :, None, :], (B, L, MLA_HEADS, MLA_ROPE))], axis=-1)
    v = kv[..., MLA_NOPE:]
    q = rms_norm(q, qk_norm_q)
    k = rms_norm(k, qk_norm_k)
    if rope is not None:
        q = jnp.concatenate([q[..., :MLA_NOPE], apply_axial_rope(q[..., MLA_NOPE:], rope)], axis=-1)
        k = jnp.concatenate([k[..., :MLA_NOPE], apply_axial_rope(k[..., MLA_NOPE:], rope)], axis=-1)
    return q[:, :, :, None, :], k, v


def gqa_qkv(u, qk_norm_q, qk_norm_k, rope):
    B, L = u.shape[:2]
    nq = GQA_Q_HEADS * GQA_HEAD_DIM
    nk = GQA_KV_HEADS * GQA_HEAD_DIM
    q = rms_norm(u[..., :nq].reshape(B, L, GQA_KV_HEADS, GQA_GROUP, GQA_HEAD_DIM), qk_norm_q)
    k = rms_norm(u[..., nq:nq + nk].reshape(B, L, GQA_KV_HEADS, GQA_HEAD_DIM), qk_norm_k)
    v = u[..., nq + nk:].reshape(B, L, GQA_KV_HEADS, GQA_HEAD_DIM)
    if rope is not None:
        q = apply_axial_rope(q, rope)
        k = apply_axial_rope(k, rope)
    return q, k, v


def mlstm_scan(q, k, v, log_i, log_f, state):
    B, L, H, d = q.shape
    nc = L // ML_CHUNK
    f32 = jnp.float32

    def to_chunks(a):
        return jnp.moveaxis(a.reshape((B, nc, ML_CHUNK) + a.shape[2:]), 1, 0)

    xs = (to_chunks(q.astype(f32)), to_chunks(k.astype(f32) * (d ** -0.5)), to_chunks(v.astype(f32)),
          to_chunks(log_i), to_chunks(log_f))
    lower = jnp.tril(jnp.ones((ML_CHUNK, ML_CHUNK), dtype=bool))

    def step(carry, inp):
        C, n, m = carry
        qc, kc, vc, ic, fc = inp
        b = jnp.cumsum(fc, axis=1).transpose(0, 2, 1)
        ih = ic.transpose(0, 2, 1)
        dmat = jnp.where(lower, b[..., :, None] - b[..., None, :] + ih[..., None, :], -jnp.inf)
        inter = b + m[..., None]
        m_t = jnp.maximum(inter, jnp.max(dmat, axis=-1))
        w_intra = jnp.exp(dmat - m_t[..., None])
        w_state = jnp.exp(inter - m_t)
        a = w_intra * jnp.einsum('bthd,bshd->bhts', qc, kc)
        num = (jnp.einsum('bhts,bshd->bthd', a, vc)
               + jnp.einsum('bht,bthk,bhkv->bthv', w_state, qc, C))
        den = jnp.sum(a, axis=-1) + w_state * jnp.einsum('bthk,bhk->bht', qc, n)
        h = num / jnp.maximum(jnp.abs(den), jnp.exp(-m_t)).transpose(0, 2, 1)[..., None]
        b_last = b[..., -1]
        g = b_last[..., None] - b + ih
        m_new = jnp.maximum(b_last + m, jnp.max(g, axis=-1))
        w_s = jnp.exp(g - m_new[..., None])
        w_c = jnp.exp(b_last + m - m_new)
        C = w_c[..., None, None] * C + jnp.einsum('bhs,bshk,bshv->bhkv', w_s, kc, vc)
        n = w_c[..., None] * n + jnp.einsum('bhs,bshk->bhk', w_s, kc)
        return (C, n, m_new), h

    state, hs = lax.scan(step, state, xs)
    return jnp.moveaxis(hs, 0, 1).reshape(B, L, H, d), state


def mlstm_inputs(u, gate_b):
    B, L = u.shape[:2]
    q, k, v, o = [a.reshape(B, L, ML_HEADS, ML_HEAD_DIM) for a in jnp.split(u[..., :4 * ML_WIDTH], 4, axis=-1)]
    gates = u[..., 4 * ML_WIDTH:].astype(jnp.float32).reshape(B, L, 4, ML_HEADS) + gate_b.astype(jnp.float32)
    return (q, k, v, o, gates[:, :, 0], gates[:, :, 1],
            jax.nn.log_sigmoid(gates[:, :, 2]), jax.nn.log_sigmoid(gates[:, :, 3]))


def bidir_mlstm(q, k, v, i_f, i_b, lf_f, lf_b, st_f, st_b):
    h_f, st_f = mlstm_scan(q, k, v, i_f, lf_f, st_f)
    flip = lambda a: jnp.flip(a, axis=1)
    h_b, st_b = mlstm_scan(flip(q), flip(k), flip(v), flip(i_b), flip(lf_b), st_b)
    return h_f + flip(h_b), st_f, st_b


def mlstm_mixer(u_ctx, u_lat, gate_b, norm_g, with_ctx):
    B = u_lat.shape[0]
    f32 = jnp.float32
    zero = (jnp.zeros((B, ML_HEADS, ML_HEAD_DIM, ML_HEAD_DIM), f32),
            jnp.zeros((B, ML_HEADS, ML_HEAD_DIM), f32), jnp.zeros((B, ML_HEADS), f32))
    qc, kc, vc, oc, icf, icb, fcf, fcb = mlstm_inputs(u_ctx, gate_b)
    h_c, st_f, st_b = bidir_mlstm(qc, kc, vc, icf, icb, fcf, fcb, zero, zero)
    ql, kl, vl, ol, ilf, ilb, flf, flb = mlstm_inputs(u_lat, gate_b)
    h_l, _, _ = bidir_mlstm(ql, kl, vl, ilf, ilb, flf, flb, st_f, st_b)

    def out(h, o):
        Bq, L = o.shape[:2]
        y = jax.nn.sigmoid(o.astype(f32)) * rms_norm(h, norm_g)
        return y.reshape(Bq, L, ML_WIDTH).astype(o.dtype)

    return (out(h_c, oc) if with_ctx else None), out(h_l, ol)


def hyena_filters(L, w1, b1, freq, w2, b2, w3):
    f32 = jnp.float32
    t = jnp.arange(L, dtype=f32) / L
    ang = 2.0 * math.pi * t[:, None] * jnp.arange(1, HY_BANDS + 1, dtype=f32)
    z = jnp.concatenate([t[:, None], jnp.sin(ang), jnp.cos(ang)], axis=-1)
    hdn = jnp.sin(freq[0].astype(f32) * (z @ w1.astype(f32) + b1.astype(f32)))
    hdn = jnp.sin(freq[1].astype(f32) * (hdn @ w2.astype(f32) + b2.astype(f32)))
    filt = (hdn @ w3.astype(f32)).reshape(L, HY_ORDER, 2, HY_WIDTH)
    log_target = math.log(HY_DECAY_TARGET)
    alpha = jnp.linspace(-log_target / HY_LONG_PCT, -log_target / HY_SHORT_PCT, HY_WIDTH, dtype=f32)
    filt = filt * jnp.exp(-t[:, None] * alpha)[:, None, None, :]
    return filt * lax.rsqrt(jnp.sum(filt * filt, axis=(0, 2), keepdims=True) + EPS)


def two_sided_fftconv(z, h_fwd, h_bwd):
    L, C = h_fwd.shape
    h_circ = jnp.concatenate([h_fwd, jnp.zeros((1, C), h_fwd.dtype), h_bwd[:0:-1]], axis=0)
    hf = jnp.fft.rfft(h_circ, n=2 * L, axis=0)
    zf = jnp.fft.rfft(z.astype(jnp.float32), n=2 * L, axis=1)
    y = jnp.fft.irfft(zf * hf[None], n=2 * L, axis=1)[:, :L]
    return y.astype(z.dtype)


def short_conv3(u, w):
    up = jnp.pad(u, ((0, 0), (1, 1), (0, 0)))
    return up[:, :-2] * w[0] + up[:, 1:-1] * w[1] + up[:, 2:] * w[2]


def hyena_mixer(u, conv_w, filt, bias):
    u = short_conv3(u, conv_w)
    v, x1, x2 = jnp.split(u, 3, axis=-1)
    z = v
    for order, gate in enumerate((x1, x2)):
        z = gate * (two_sided_fftconv(z, filt[:, order, 0], filt[:, order, 1]) + z * bias[order])
    return z


def merge_branches(ys, gate_logits, w_branch, w_out):
    B, L = gate_logits.shape[:2]
    gl = gate_logits.reshape(B, L, N_BRANCH, -1)
    acc = jax.nn.sigmoid(gl[:, :, 0]) * (ys[0] @ w_branch[0])
    for i in range(1, N_BRANCH):
        acc = acc + jax.nn.sigmoid(gl[:, :, i]) * (ys[i] @ w_branch[i])
    return acc @ w_out


def peer(h, wq, subkeys, u_tab, v_tab):
    T, D = h.shape
    q = (h @ wq).reshape(T, PEER_HEADS, 2, PEER_DK // 2)
    s = jnp.einsum('thcd,hcnd->thcn', q, subkeys, preferred_element_type=jnp.float32)
    sv, si = lax.top_k(s, PEER_TOPK)
    cand = (sv[:, :, 0, :, None] + sv[:, :, 1, None, :]).reshape(T, PEER_HEADS, PEER_TOPK * PEER_TOPK)
    cidx = (si[:, :, 0, :, None] * PEER_NKEYS + si[:, :, 1, None, :]).reshape(T, PEER_HEADS, PEER_TOPK * PEER_TOPK)
    best, pos = lax.top_k(cand, PEER_TOPK)
    idx = jnp.take_along_axis(cidx, pos, axis=-1)
    g = jax.nn.softmax(best, axis=-1)
    nb = T // PEER_BLOCK

    def block(args):
        hb, ib, gb = args
        act = jax.nn.gelu(jnp.einsum('thkd,td->thk', u_tab[ib], hb, preferred_element_type=jnp.float32))
        return jnp.einsum('thk,thkd->td', (gb * act).astype(v_tab.dtype), v_tab[ib])

    out = lax.map(block, (h.reshape(nb, PEER_BLOCK, D),
                          idx.reshape(nb, PEER_BLOCK, PEER_HEADS, PEER_TOPK),
                          g.reshape(nb, PEER_BLOCK, PEER_HEADS, PEER_TOPK)))
    return out.reshape(T, D).astype(h.dtype)


def trunk_layer(xc, xl, mod_c, mod_l, rope_a, rope_b, with_ctx, norm1, norm2, w_in,
                mla_q_norm, mla_kv_norm, mla_w_uq, mla_w_ukv, mla_qk_norm_q, mla_qk_norm_k,
                gqa_qk_norm_q, gqa_qk_norm_k, ml_gate_b, ml_norm,
                hy_conv, hy_w1, hy_b1, hy_freq, hy_w2, hy_b2, hy_w3, hy_bias,
                w_branch, w_out, peer_wq, peer_subkeys, peer_u, peer_v):
    D = xl.shape[-1]
    sh1_c, sc1_c, g1_c, sh2_c, sc2_c, g2_c = jnp.split(mod_c, 6, axis=-1)
    sh1_l, sc1_l, g1_l, sh2_l, sc2_l, g2_l = [a[:, None, :] for a in jnp.split(mod_l, 6, axis=-1)]

    pc = modulate(xc, norm1, sh1_c, sc1_c) @ w_in
    pl_ = modulate(xl, norm1, sh1_l, sc1_l) @ w_in
    a_c, b_c, m_c, h_c, gate_c = jnp.split(pc, IN_OFFSETS, axis=-1)
    a_l, b_l, m_l, h_l, gate_l = jnp.split(pl_, IN_OFFSETS, axis=-1)

    qa_c, ka_c, va_c = mla_qkv(a_c, mla_q_norm, mla_kv_norm, mla_w_uq, mla_w_ukv, mla_qk_norm_q, mla_qk_norm_k, None)
    qa_l, ka_l, va_l = mla_qkv(a_l, mla_q_norm, mla_kv_norm, mla_w_uq, mla_w_ukv, mla_qk_norm_q, mla_qk_norm_k, rope_a)
    ya_l = latent_attention(qa_l, ka_c, va_c, ka_l, va_l, MLA_QK ** -0.5)
    qb_c, kb_c, vb_c = gqa_qkv(b_c, gqa_qk_norm_q, gqa_qk_norm_k, None)
    qb_l, kb_l, vb_l = gqa_qkv(b_l, gqa_qk_norm_q, gqa_qk_norm_k, rope_b)
    yb_l = latent_attention(qb_l, kb_c, vb_c, kb_l, vb_l, GQA_HEAD_DIM ** -0.5)
    yc_c, yc_l = mlstm_mixer(m_c, m_l, ml_gate_b, ml_norm, with_ctx)
    filt_l = hyena_filters(xl.shape[1], hy_w1, hy_b1, hy_freq, hy_w2, hy_b2, hy_w3)
    yd_l = hyena_mixer(h_l, hy_conv, filt_l, hy_bias)

    xl = xl + g1_l * merge_branches([ya_l, yb_l, yc_l, yd_l], gate_l, w_branch, w_out)
    if with_ctx:
        Bc, Lc = xc.shape[:2]
        ya_c = attend(qa_c, ka_c, va_c, MLA_QK ** -0.5).reshape(Bc, Lc, -1)
        yb_c = attend(qb_c, kb_c, vb_c, GQA_HEAD_DIM ** -0.5).reshape(Bc, Lc, -1)
        filt_c = hyena_filters(Lc, hy_w1, hy_b1, hy_freq, hy_w2, hy_b2, hy_w3)
        yd_c = hyena_mixer(h_c, hy_conv, filt_c, hy_bias)
        xc = xc + g1_c * merge_branches([ya_c, yb_c, yc_c, yd_c], gate_c, w_branch, w_out)

    h2l = modulate(xl, norm2, sh2_l, sc2_l)
    if with_ctx:
        h2c = modulate(xc, norm2, sh2_c, sc2_c)
        n_ctx = xc.shape[0] * xc.shape[1]
        f = peer(jnp.concatenate([h2c.reshape(-1, D), h2l.reshape(-1, D)], axis=0), peer_wq, peer_subkeys, peer_u, peer_v)
        xc = xc + g2_c * f[:n_ctx].reshape(xc.shape)
        xl = xl + g2_l * f[n_ctx:].reshape(xl.shape)
    else:
        xl = xl + g2_l * peer(h2l.reshape(-1, D), peer_wq, peer_subkeys, peer_u, peer_v).reshape(xl.shape)
    return xc, xl


def kernel(x, c, ctx, c_ctx, ada_w, ada_b, norm1, norm2, w_in,
           mla_q_norm, mla_kv_norm, mla_w_uq, mla_w_ukv, mla_qk_norm_q, mla_qk_norm_k,
           gqa_qk_norm_q, gqa_qk_norm_k, ml_gate_b, ml_norm,
           hy_conv, hy_w1, hy_b1, hy_freq, hy_w2, hy_b2, hy_w3, hy_bias,
           w_branch, w_out, peer_wq, peer_subkeys, peer_u, peer_v):
    B, S, D = x.shape
    ROWS = S // GRID_W
    rows = jnp.broadcast_to(jnp.arange(ROWS, dtype=jnp.int32)[:, None], (ROWS, GRID_W)).reshape(-1)
    cols = jnp.broadcast_to(jnp.arange(GRID_W, dtype=jnp.int32)[None, :], (ROWS, GRID_W)).reshape(-1)
    rope_a = axial_rope_tables(rows, cols, MLA_ROPE)
    rope_b = axial_rope_tables(rows, cols, GQA_HEAD_DIM)
    s_lat = jax.nn.silu(c)
    s_ctx = jax.nn.silu(c_ctx)
    xc, xl = ctx, x
    for l in range(DEPTH):
        mod_l = s_lat @ ada_w[l] + ada_b[l]
        mod_c = s_ctx @ ada_w[l] + ada_b[l]
        xc, xl = trunk_layer(
            xc, xl, mod_c, mod_l, rope_a, rope_b, l < DEPTH - 1, norm1[l], norm2[l], w_in[l],
            mla_q_norm[l], mla_kv_norm[l], mla_w_uq[l], mla_w_ukv[l], mla_qk_norm_q[l], mla_qk_norm_k[l],
            gqa_qk_norm_q[l], gqa_qk_norm_k[l], ml_gate_b[l], ml_norm[l],
            hy_conv[l], hy_w1[l], hy_b1[l], hy_freq[l], hy_w2[l], hy_b2[l], hy_w3[l], hy_bias[l],
            w_branch[l], w_out[l], peer_wq[l], peer_subkeys[l], peer_u[l], peer_v[l])
    return xl
```

```python
import functools
import math

import jax
import jax.numpy as jnp
from jax import lax
from jax.experimental import pallas as pl
from jax.experimental.pallas import tpu as pltpu

D_MODEL = 1024
DEPTH = 4
GRID_W = 64
ROPE_THETA = 10000.0
EPS = 1e-6

MLA_HEADS = 4
MLA_Q_LORA = 256
MLA_KV_LORA = 128
MLA_NOPE = 64
MLA_ROPE = 32
MLA_V = 64
MLA_QK = MLA_NOPE + MLA_ROPE
MLA_IN = MLA_Q_LORA + MLA_KV_LORA + MLA_ROPE

GQA_Q_HEADS = 4
GQA_KV_HEADS = 2
GQA_GROUP = GQA_Q_HEADS // GQA_KV_HEADS
GQA_HEAD_DIM = 64
GQA_IN = (GQA_Q_HEADS + 2 * GQA_KV_HEADS) * GQA_HEAD_DIM

ML_HEADS = 4
ML_HEAD_DIM = 64
ML_WIDTH = ML_HEADS * ML_HEAD_DIM
ML_CHUNK = 64
ML_IN = 4 * ML_WIDTH + 4 * ML_HEADS

HY_WIDTH = 256
HY_ORDER = 2
HY_BANDS = 16
HY_FEAT = 1 + 2 * HY_BANDS
HY_HIDDEN = 64
HY_IN = (HY_ORDER + 1) * HY_WIDTH
HY_DECAY_TARGET = 1e-2
HY_SHORT_PCT = 0.3
HY_LONG_PCT = 1.5

N_BRANCH = 4
BRANCH_W = 256
GATE_IN = N_BRANCH * D_MODEL
IN_OFFSETS = (MLA_IN, MLA_IN + GQA_IN, MLA_IN + GQA_IN + ML_IN, MLA_IN + GQA_IN + ML_IN + HY_IN)

PEER_HEADS = 8
PEER_NKEYS = 128
PEER_DK = 256
PEER_TOPK = 16
PEER_BLOCK = 128

V7X_LANES = 128
V7X_VMEM_BYTES = 64 * 1024 * 1024

MXU_DTYPE = jnp.bfloat16

PEER_TN = 512
PEER_TI = 8
PEER_TE = PEER_TI * PEER_NKEYS

ATT_TQ = 256
ATT_TK = 1280


def _flash_kernel(q_ref, kt_ref, v_ref, o_ref, *, tk, nk):
    q = q_ref[0, 0]
    tq = q.shape[0]
    dv = v_ref.shape[-1]

    def body(j, carry):
        m, l, acc = carry
        start = pl.multiple_of(j * tk, tk)
        kt = kt_ref[0, 0, :, pl.ds(start, tk)]
        s = jnp.dot(q, kt, preferred_element_type=jnp.float32)
        m_new = jnp.maximum(m, jnp.max(s, axis=-1, keepdims=True))
        alpha = jnp.exp(m - m_new)
        p = jnp.exp(s - m_new)
        l = alpha * l + jnp.sum(p, axis=-1, keepdims=True)
        vv = v_ref[0, 0, pl.ds(start, tk), :]
        acc = alpha * acc + jnp.dot(p.astype(vv.dtype), vv, preferred_element_type=jnp.float32)
        return m_new, l, acc

    m0 = jnp.full((tq, 1), -jnp.inf, jnp.float32)
    l0 = jnp.zeros((tq, 1), jnp.float32)
    acc0 = jnp.zeros((tq, dv), jnp.float32)
    _, l, acc = lax.fori_loop(0, nk, body, (m0, l0, acc0))
    o_ref[0, 0] = (acc / l).astype(o_ref.dtype)


def flash_attention(q, kt, v, group):
    B, H, S, d = q.shape
    K = kt.shape[-1]
    dv = v.shape[-1]
    assert S % ATT_TQ == 0 and K % ATT_TK == 0
    nk = K // ATT_TK
    return pl.pallas_call(
        functools.partial(_flash_kernel, tk=ATT_TK, nk=nk),
        grid=(B, H, S // ATT_TQ),
        in_specs=[
            pl.BlockSpec((1, 1, ATT_TQ, d), lambda b, h, i: (b, h, i, 0)),
            pl.BlockSpec((1, 1, d, K), lambda b, h, i: (b, h // group, 0, 0)),
            pl.BlockSpec((1, 1, K, dv), lambda b, h, i: (b, h // group, 0, 0)),
        ],
        out_specs=pl.BlockSpec((1, 1, ATT_TQ, dv), lambda b, h, i: (b, h, i, 0)),
        out_shape=jax.ShapeDtypeStruct((B, H, S, dv), jnp.float32),
        compiler_params=pltpu.CompilerParams(
            dimension_semantics=("arbitrary", "arbitrary", "arbitrary"),
            vmem_limit_bytes=V7X_VMEM_BYTES * 3 // 4),
        name="flash_attention",
    )(q, kt, v)


def latent_attention(q_lat, k_ctx, v_ctx, k_lat, v_lat, scale):
    B, S, Hk, G, dk = q_lat.shape
    k_all = jnp.concatenate([k_ctx, k_lat], axis=1)
    v_all = jnp.concatenate([v_ctx, v_lat], axis=1)
    q = (q_lat * scale).astype(jnp.bfloat16).reshape(B, S, Hk * G, dk).transpose(0, 2, 1, 3)
    kt = k_all.astype(jnp.bfloat16).transpose(0, 2, 3, 1)
    v = v_all.astype(jnp.bfloat16).transpose(0, 2, 1, 3)
    o = flash_attention(q, kt, v, G)
    return o.transpose(0, 2, 1, 3).reshape(B, S, -1)


def rms_norm(x, g):
    xf = x.astype(jnp.float32)
    y = xf * lax.rsqrt(jnp.mean(xf * xf, axis=-1, keepdims=True) + EPS)
    return (y * g.astype(jnp.float32)).astype(x.dtype)


def modulate(x, g, shift, scale):
    return rms_norm(x, g) * (1.0 + scale) + shift


def axial_rope_tables(rows, cols, d_rot):
    m = d_rot // 2
    inv = ROPE_THETA ** (-jnp.arange(0, m, 2, dtype=jnp.float32) / m)
    ar = rows.astype(jnp.float32)[:, None] * inv
    ac = cols.astype(jnp.float32)[:, None] * inv
    return (jnp.cos(ar), jnp.sin(ar), jnp.cos(ac), jnp.sin(ac))


def _rotate(x, cos, sin):
    x1, x2 = jnp.split(x, 2, axis=-1)
    return jnp.concatenate([x1 * cos - x2 * sin, x2 * cos + x1 * sin], axis=-1)


def apply_axial_rope(x, tables):
    extra = x.ndim - 3
    t = [a.reshape((a.shape[0],) + (1,) * extra + (a.shape[1],)) for a in tables]
    xr, xc = jnp.split(x.astype(jnp.float32), 2, axis=-1)
    out = jnp.concatenate([_rotate(xr, t[0], t[1]), _rotate(xc, t[2], t[3])], axis=-1)
    return out.astype(x.dtype)


def attend(q, k, v, scale):
    s = jnp.einsum('bqhgd,bkhd->bhgqk', q, k, preferred_element_type=jnp.float32) * scale
    p = jax.nn.softmax(s, axis=-1).astype(v.dtype)
    return jnp.einsum('bhgqk,bkhd->bqhgd', p, v)


def mla_qkv(u, q_norm, kv_norm, w_uq, w_ukv, qk_norm_q, qk_norm_k, rope):
    B, L = u.shape[:2]
    c_q = u[..., :MLA_Q_LORA]
    c_kv = u[..., MLA_Q_LORA:MLA_Q_LORA + MLA_KV_LORA]
    k_pe = u[..., MLA_Q_LORA + MLA_KV_LORA:]
    q = (rms_norm(c_q, q_norm) @ w_uq).reshape(B, L, MLA_HEADS, MLA_QK)
    kv = (rms_norm(c_kv, kv_norm) @ w_ukv).reshape(B, L, MLA_HEADS, MLA_NOPE + MLA_V)
    k = jnp.concatenate([kv[..., :MLA_NOPE],
                         jnp.broadcast_to(k_pe[:, :, None, :], (B, L, MLA_HEADS, MLA_ROPE))], axis=-1)
    v = kv[..., MLA_NOPE:]
    q = rms_norm(q, qk_norm_q)
    k = rms_norm(k, qk_norm_k)
    if rope is not None:
        q = jnp.concatenate([q[..., :MLA_NOPE], apply_axial_rope(q[..., MLA_NOPE:], rope)], axis=-1)
        k = jnp.concatenate([k[..., :MLA_NOPE], apply_axial_rope(k[..., MLA_NOPE:], rope)], axis=-1)
    return q[:, :, :, None, :], k, v


def gqa_qkv(u, qk_norm_q, qk_norm_k, rope):
    B, L = u.shape[:2]
    nq = GQA_Q_HEADS * GQA_HEAD_DIM
    nk = GQA_KV_HEADS * GQA_HEAD_DIM
    q = rms_norm(u[..., :nq].reshape(B, L, GQA_KV_HEADS, GQA_GROUP, GQA_HEAD_DIM), qk_norm_q)
    k = rms_norm(u[..., nq:nq + nk].reshape(B, L, GQA_KV_HEADS, GQA_HEAD_DIM), qk_norm_k)
    v = u[..., nq + nk:].reshape(B, L, GQA_KV_HEADS, GQA_HEAD_DIM)
    if rope is not None:
        q = apply_axial_rope(q, rope)
        k = apply_axial_rope(k, rope)
    return q, k, v


def mlstm_scan(q, k, v, log_i, log_f, state):
    B, L, H, d = q.shape
    nc = L // ML_CHUNK
    f32 = jnp.float32

    def to_chunks(a):
        return jnp.moveaxis(a.reshape((B, nc, ML_CHUNK) + a.shape[2:]), 1, 0)

    xs = (to_chunks(q.astype(f32)), to_chunks(k.astype(f32) * (d ** -0.5)), to_chunks(v.astype(f32)),
          to_chunks(log_i), to_chunks(log_f))
    lower = jnp.tril(jnp.ones((ML_CHUNK, ML_CHUNK), dtype=bool))

    def step(carry, inp):
        C, n, m = carry
        qc, kc, vc, ic, fc = inp
        b = jnp.cumsum(fc, axis=1).transpose(0, 2, 1)
        ih = ic.transpose(0, 2, 1)
        dmat = jnp.where(lower, b[..., :, None] - b[..., None, :] + ih[..., None, :], -jnp.inf)
        inter = b + m[..., None]
        m_t = jnp.maximum(inter, jnp.max(dmat, axis=-1))
        w_intra = jnp.exp(dmat - m_t[..., None])
        w_state = jnp.exp(inter - m_t)
        a = w_intra * jnp.einsum('bthd,bshd->bhts', qc, kc)
        num = (jnp.einsum('bhts,bshd->bthd', a, vc)
               + jnp.einsum('bht,bthk,bhkv->bthv', w_state, qc, C))
        den = jnp.sum(a, axis=-1) + w_state * jnp.einsum('bthk,bhk->bht', qc, n)
        h = num / jnp.maximum(jnp.abs(den), jnp.exp(-m_t)).transpose(0, 2, 1)[..., None]
        b_last = b[..., -1]
        g = b_last[..., None] - b + ih
        m_new = jnp.maximum(b_last + m, jnp.max(g, axis=-1))
        w_s = jnp.exp(g - m_new[..., None])
        w_c = jnp.exp(b_last + m - m_new)
        C = w_c[..., None, None] * C + jnp.einsum('bhs,bshk,bshv->bhkv', w_s, kc, vc)
        n = w_c[..., None] * n + jnp.einsum('bhs,bshk->bhk', w_s, kc)
        return (C, n, m_new), h

    state, hs = lax.scan(step, state, xs)
    return jnp.moveaxis(hs, 0, 1).reshape(B, L, H, d), state


def mlstm_inputs(u, gate_b):
    B, L = u.shape[:2]
    q, k, v, o = [a.reshape(B, L, ML_HEADS, ML_HEAD_DIM) for a in jnp.split(u[..., :4 * ML_WIDTH], 4, axis=-1)]
    gates = u[..., 4 * ML_WIDTH:].astype(jnp.float32).reshape(B, L, 4, ML_HEADS) + gate_b.astype(jnp.float32)
    return (q, k, v, o, gates[:, :, 0], gates[:, :, 1],
            jax.nn.log_sigmoid(gates[:, :, 2]), jax.nn.log_sigmoid(gates[:, :, 3]))


def bidir_mlstm(q, k, v, i_f, i_b, lf_f, lf_b, st_f, st_b):
    h_f, st_f = mlstm_scan(q, k, v, i_f, lf_f, st_f)
    flip = lambda a: jnp.flip(a, axis=1)
    h_b, st_b = mlstm_scan(flip(q), flip(k), flip(v), flip(i_b), flip(lf_b), st_b)
    return h_f + flip(h_b), st_f, st_b


def mlstm_mixer(u_ctx, u_lat, gate_b, norm_g, with_ctx):
    B = u_lat.shape[0]
    f32 = jnp.float32
    zero = (jnp.zeros((B, ML_HEADS, ML_HEAD_DIM, ML_HEAD_DIM), f32),
            jnp.zeros((B, ML_HEADS, ML_HEAD_DIM), f32), jnp.zeros((B, ML_HEADS), f32))
    qc, kc, vc, oc, icf, icb, fcf, fcb = mlstm_inputs(u_ctx, gate_b)
    h_c, st_f, st_b = bidir_mlstm(qc, kc, vc, icf, icb, fcf, fcb, zero, zero)
    ql, kl, vl, ol, ilf, ilb, flf, flb = mlstm_inputs(u_lat, gate_b)
    h_l, _, _ = bidir_mlstm(ql, kl, vl, ilf, ilb, flf, flb, st_f, st_b)

    def out(h, o):
        Bq, L = o.shape[:2]
        y = jax.nn.sigmoid(o.astype(f32)) * rms_norm(h, norm_g)
        return y.reshape(Bq, L, ML_WIDTH).astype(o.dtype)

    return (out(h_c, oc) if with_ctx else None), out(h_l, ol)


def hyena_filters(L, w1, b1, freq, w2, b2, w3):
    f32 = jnp.float32
    t = jnp.arange(L, dtype=f32) / L
    ang = 2.0 * math.pi * t[:, None] * jnp.arange(1, HY_BANDS + 1, dtype=f32)
    z = jnp.concatenate([t[:, None], jnp.sin(ang), jnp.cos(ang)], axis=-1)
    hdn = jnp.sin(freq[0].astype(f32) * (z @ w1.astype(f32) + b1.astype(f32)))
    hdn = jnp.sin(freq[1].astype(f32) * (hdn @ w2.astype(f32) + b2.astype(f32)))
    filt = (hdn @ w3.astype(f32)).reshape(L, HY_ORDER, 2, HY_WIDTH)
    log_target = math.log(HY_DECAY_TARGET)
    alpha = jnp.linspace(-log_target / HY_LONG_PCT, -log_target / HY_SHORT_PCT, HY_WIDTH, dtype=f32)
    filt = filt * jnp.exp(-t[:, None] * alpha)[:, None, None, :]
    return filt * lax.rsqrt(jnp.sum(filt * filt, axis=(0, 2), keepdims=True) + EPS)


def two_sided_fftconv(z, h_fwd, h_bwd):
    L, C = h_fwd.shape
    h_circ = jnp.concatenate([h_fwd, jnp.zeros((1, C), h_fwd.dtype), h_bwd[:0:-1]], axis=0)
    hf = jnp.fft.rfft(h_circ, n=2 * L, axis=0)
    zf = jnp.fft.rfft(z.astype(jnp.float32), n=2 * L, axis=1)
    y = jnp.fft.irfft(zf * hf[None], n=2 * L, axis=1)[:, :L]
    return y.astype(z.dtype)


def short_conv3(u, w):
    up = jnp.pad(u, ((0, 0), (1, 1), (0, 0)))
    return up[:, :-2] * w[0] + up[:, 1:-1] * w[1] + up[:, 2:] * w[2]


def hyena_mixer(u, conv_w, filt, bias):
    u = short_conv3(u, conv_w)
    v, x1, x2 = jnp.split(u, 3, axis=-1)
    z = v
    for order, gate in enumerate((x1, x2)):
        z = gate * (two_sided_fftconv(z, filt[:, order, 0], filt[:, order, 1]) + z * bias[order])
    return z


def merge_branches(ys, gate_logits, w_branch, w_out):
    B, L = gate_logits.shape[:2]
    gl = gate_logits.reshape(B, L, N_BRANCH, -1)
    acc = jax.nn.sigmoid(gl[:, :, 0]) * (ys[0] @ w_branch[0])
    for i in range(1, N_BRANCH):
        acc = acc + jax.nn.sigmoid(gl[:, :, i]) * (ys[i] @ w_branch[i])
    return acc @ w_out


def _topk_rows(s, k):
    rows = []
    for _ in range(k):
        m = jnp.max(s, axis=0, keepdims=True)
        rows.append(m)
        s = jnp.where(s == m, -jnp.inf, s)
    return rows


def _peer_route_kernel(ht_ref, wqt_ref, sk_ref, s1_ref, s2_ref, ea_ref, eb_ref, thr_ref):
    f32 = jnp.float32
    tn = ht_ref.shape[1]
    half = PEER_DK // 2
    qt = jnp.dot(wqt_ref[...], ht_ref[...], preferred_element_type=f32).astype(sk_ref.dtype)
    s1 = jnp.dot(sk_ref[0, 0], qt[:half], preferred_element_type=f32)
    s2 = jnp.dot(sk_ref[0, 1], qt[half:], preferred_element_type=f32)
    sv1 = _topk_rows(s1, PEER_TOPK)
    sv2 = _topk_rows(s2, PEER_TOPK)
    sv2_stack = jnp.concatenate(sv2, axis=0)
    row = lax.broadcasted_iota(jnp.int32, (8, tn), 0)
    tiles = [sv1[0] + sv2_stack]
    for a in range(1, PEER_TOPK):
        nb = PEER_TOPK // (a + 1)
        t = sv1[a] + sv2_stack[:8]
        if nb < 8:
            t = jnp.where(row < nb, t, -jnp.inf)
        tiles.append(t)
    cand = jnp.concatenate(tiles, axis=0)
    cmax = sv1[0] + sv2[0]
    thr = _topk_rows(cand, PEER_TOPK)[-1]
    z = jnp.sum(jnp.where(cand >= thr, jnp.exp(cand - cmax), 0.0), axis=0, keepdims=True)
    s1_ref[0] = s1
    s2_ref[0] = s2
    ea_ref[0] = jnp.exp(s1 - sv1[0]) / z
    eb_ref[0] = jnp.exp(s2 - sv2[0])
    thr_ref[0] = thr


def _gelu_tanh(x):
    return x * (0.5 * (1.0 + jnp.tanh(0.7978845608028654 * (x + 0.044715 * (x * x * x)))))


def _peer_expert_kernel(ht_ref, u_ref, vt_ref, s1_ref, s2_ref, ea_ref, eb_ref, thr_ref, o_ref, w_ref):
    f32 = jnp.float32
    e = pl.program_id(1)
    tn = ht_ref.shape[1]

    @pl.when(e == 0)
    def _init():
        o_ref[...] = jnp.zeros_like(o_ref)

    for ii in range(PEER_TI):
        rows = slice(ii * PEER_NKEYS, (ii + 1) * PEER_NKEYS)
        for tc in range(tn // V7X_LANES):
            cols = slice(tc * V7X_LANES, (tc + 1) * V7X_LANES)
            w = jnp.zeros((PEER_NKEYS, V7X_LANES), f32)
            for h in range(PEER_HEADS):
                s = s2_ref[h, :, cols] + s1_ref[h, ii:ii + 1, cols]
                w = w + jnp.where(s >= thr_ref[h, :, cols], eb_ref[h, :, cols], 0.0) * ea_ref[h, ii:ii + 1, cols]
            w_ref[rows, cols] = w
    a = jnp.dot(u_ref[...], ht_ref[...], preferred_element_type=f32)
    wa = (w_ref[...] * _gelu_tanh(a)).astype(vt_ref.dtype)
    o_ref[...] += jnp.dot(vt_ref[...], wa, preferred_element_type=f32)


def peer(h, wq, subkeys, u_tab, v_tab):
    T, D = h.shape
    assert T % PEER_TN == 0
    nt = T // PEER_TN
    ne = PEER_NKEYS // PEER_TI
    f32 = jnp.float32
    ht = h.T.astype(MXU_DTYPE)
    wqt = wq.T.astype(MXU_DTYPE)
    sk = subkeys.astype(MXU_DTYPE)
    u = u_tab.astype(MXU_DTYPE)
    vt = v_tab.T.astype(MXU_DTYPE)
    head_blk = pl.BlockSpec((1, PEER_NKEYS, PEER_TN), lambda n, hh: (hh, 0, n))
    head_shape = jax.ShapeDtypeStruct((PEER_HEADS, PEER_NKEYS, T), f32)
    s1, s2, ea, eb, thr = pl.pallas_call(
        _peer_route_kernel,
        grid=(nt, PEER_HEADS),
        in_specs=[
            pl.BlockSpec((D, PEER_TN), lambda n, hh: (0, n)),
            pl.BlockSpec((PEER_DK, D), lambda n, hh: (hh, 0)),
            pl.BlockSpec((1, 2, PEER_NKEYS, PEER_DK // 2), lambda n, hh: (hh, 0, 0, 0)),
        ],
        out_specs=[head_blk, head_blk, head_blk, head_blk,
                   pl.BlockSpec((1, 1, PEER_TN), lambda n, hh: (hh, 0, n))],
        out_shape=[head_shape, head_shape, head_shape, head_shape,
                   jax.ShapeDtypeStruct((PEER_HEADS, 1, T), f32)],
        compiler_params=pltpu.CompilerParams(
            dimension_semantics=("arbitrary", "arbitrary"),
            vmem_limit_bytes=V7X_VMEM_BYTES * 3 // 4),
        name="peer_route",
    )(ht, wqt, sk)

    tok_blk = pl.BlockSpec((PEER_HEADS, PEER_NKEYS, PEER_TN), lambda n, e: (0, 0, n))
    row_blk = pl.BlockSpec((PEER_HEADS, PEER_TI, PEER_TN), lambda n, e: (0, e, n))
    out_t = pl.pallas_call(
        _peer_expert_kernel,
        grid=(nt, ne),
        in_specs=[
            pl.BlockSpec((D, PEER_TN), lambda n, e: (0, n)),
            pl.BlockSpec((PEER_TE, D), lambda n, e: (e, 0)),
            pl.BlockSpec((D, PEER_TE), lambda n, e: (0, e)),
            row_blk, tok_blk, row_blk, tok_blk,
            pl.BlockSpec((PEER_HEADS, 1, PEER_TN), lambda n, e: (0, 0, n)),
        ],
        out_specs=pl.BlockSpec((D, PEER_TN), lambda n, e: (0, n)),
        out_shape=jax.ShapeDtypeStruct((D, T), f32),
        scratch_shapes=[pltpu.VMEM((PEER_TE, PEER_TN), f32)],
        compiler_params=pltpu.CompilerParams(
            dimension_semantics=("arbitrary", "arbitrary"),
            vmem_limit_bytes=V7X_VMEM_BYTES * 3 // 4),
        name="peer_expert",
    )(ht, u, vt, s1, s2, ea, eb, thr)
    return out_t.T.astype(h.dtype)


def trunk_layer(xc, xl, mod_c, mod_l, rope_a, rope_b, with_ctx, norm1, norm2, w_in,
                mla_q_norm, mla_kv_norm, mla_w_uq, mla_w_ukv, mla_qk_norm_q, mla_qk_norm_k,
                gqa_qk_norm_q, gqa_qk_norm_k, ml_gate_b, ml_norm,
                hy_conv, hy_w1, hy_b1, hy_freq, hy_w2, hy_b2, hy_w3, hy_bias,
                w_branch, w_out, peer_wq, peer_subkeys, peer_u, peer_v):
    D = xl.shape[-1]
    sh1_c, sc1_c, g1_c, sh2_c, sc2_c, g2_c = jnp.split(mod_c, 6, axis=-1)
    sh1_l, sc1_l, g1_l, sh2_l, sc2_l, g2_l = [a[:, None, :] for a in jnp.split(mod_l, 6, axis=-1)]

    pc = modulate(xc, norm1, sh1_c, sc1_c) @ w_in
    pl_ = modulate(xl, norm1, sh1_l, sc1_l) @ w_in
    a_c, b_c, m_c, h_c, gate_c = jnp.split(pc, IN_OFFSETS, axis=-1)
    a_l, b_l, m_l, h_l, gate_l = jnp.split(pl_, IN_OFFSETS, axis=-1)

    qa_c, ka_c, va_c = mla_qkv(a_c, mla_q_norm, mla_kv_norm, mla_w_uq, mla_w_ukv, mla_qk_norm_q, mla_qk_norm_k, None)
    qa_l, ka_l, va_l = mla_qkv(a_l, mla_q_norm, mla_kv_norm, mla_w_uq, mla_w_ukv, mla_qk_norm_q, mla_qk_norm_k, rope_a)
    ya_l = latent_attention(qa_l, ka_c, va_c, ka_l, va_l, MLA_QK ** -0.5)
    qb_c, kb_c, vb_c = gqa_qkv(b_c, gqa_qk_norm_q, gqa_qk_norm_k, None)
    qb_l, kb_l, vb_l = gqa_qkv(b_l, gqa_qk_norm_q, gqa_qk_norm_k, rope_b)
    yb_l = latent_attention(qb_l, kb_c, vb_c, kb_l, vb_l, GQA_HEAD_DIM ** -0.5)
    yc_c, yc_l = mlstm_mixer(m_c, m_l, ml_gate_b, ml_norm, with_ctx)
    filt_l = hyena_filters(xl.shape[1], hy_w1, hy_b1, hy_freq, hy_w2, hy_b2, hy_w3)
    yd_l = hyena_mixer(h_l, hy_conv, filt_l, hy_bias)

    xl = xl + g1_l * merge_branches([ya_l, yb_l, yc_l, yd_l], gate_l, w_branch, w_out)
    if with_ctx:
        Bc, Lc = xc.shape[:2]
        ya_c = attend(qa_c, ka_c, va_c, MLA_QK ** -0.5).reshape(Bc, Lc, -1)
        yb_c = attend(qb_c, kb_c, vb_c, GQA_HEAD_DIM ** -0.5).reshape(Bc, Lc, -1)
        filt_c = hyena_filters(Lc, hy_w1, hy_b1, hy_freq, hy_w2, hy_b2, hy_w3)
        yd_c = hyena_mixer(h_c, hy_conv, filt_c, hy_bias)
        xc = xc + g1_c * merge_branches([ya_c, yb_c, yc_c, yd_c], gate_c, w_branch, w_out)

    h2l = modulate(xl, norm2, sh2_l, sc2_l)
    if with_ctx:
        h2c = modulate(xc, norm2, sh2_c, sc2_c)
        n_ctx = xc.shape[0] * xc.shape[1]
        f = peer(jnp.concatenate([h2c.reshape(-1, D), h2l.reshape(-1, D)], axis=0), peer_wq, peer_subkeys, peer_u, peer_v)
        xc = xc + g2_c * f[:n_ctx].reshape(xc.shape)
        xl = xl + g2_l * f[n_ctx:].reshape(xl.shape)
    else:
        xl = xl + g2_l * peer(h2l.reshape(-1, D), peer_wq, peer_subkeys, peer_u, peer_v).reshape(xl.shape)
    return xc, xl


def kernel(x, c, ctx, c_ctx, ada_w, ada_b, norm1, norm2, w_in,
           mla_q_norm, mla_kv_norm, mla_w_uq, mla_w_ukv, mla_qk_norm_q, mla_qk_norm_k,
           gqa_qk_norm_q, gqa_qk_norm_k, ml_gate_b, ml_norm,
           hy_conv, hy_w1, hy_b1, hy_freq, hy_w2, hy_b2, hy_w3, hy_bias,
           w_branch, w_out, peer_wq, peer_subkeys, peer_u, peer_v):
    B, S, D = x.shape
    ROWS = S // GRID_W
    rows = jnp.broadcast_to(jnp.arange(ROWS, dtype=jnp.int32)[:, None], (ROWS, GRID_W)).reshape(-1)
    cols = jnp.broadcast_to(jnp.arange(GRID_W, dtype=jnp.int32)[None, :], (ROWS, GRID_W)).reshape(-1)
    rope_a = axial_rope_tables(rows, cols, MLA_ROPE)
    rope_b = axial_rope_tables(rows, cols, GQA_HEAD_DIM)
    s_lat = jax.nn.silu(c)
    s_ctx = jax.nn.silu(c_ctx)
    xc, xl = ctx, x
    for l in range(DEPTH):
        mod_l = s_lat @ ada_w[l] + ada_b[l]
        mod_c = s_ctx @ ada_w[l] + ada_b[l]
        xc, xl = trunk_layer(
            xc, xl, mod_c, mod_l, rope_a, rope_b, l < DEPTH - 1, norm1[l], norm2[l], w_in[l],
            mla_q_norm[l], mla_kv_norm[l], mla_w_uq[l], mla_w_ukv[l], mla_qk_norm_q[l], mla_qk_norm_k[l],
            gqa_qk_norm_q[l], gqa_qk_norm_k[l], ml_gate_b[l], ml_norm[l],
            hy_conv[l], hy_w1[l], hy_b1[l], hy_freq[l], hy_w2[l], hy_b2[l], hy_w3[l], hy_bias[l],
            w_branch[l], w_out[l], peer_wq[l], peer_subkeys[l], peer_u[l], peer_v[l])
    return xl
```

```python
import functools
import math

import jax
import jax.numpy as jnp
from jax import lax
from jax.experimental import pallas as pl
from jax.experimental.pallas import tpu as pltpu

D_MODEL = 1024
DEPTH = 4
GRID_W = 64
ROPE_THETA = 10000.0
EPS = 1e-6

MLA_HEADS = 4
MLA_Q_LORA = 256
MLA_KV_LORA = 128
MLA_NOPE = 64
MLA_ROPE = 32
MLA_V = 64
MLA_QK = MLA_NOPE + MLA_ROPE
MLA_IN = MLA_Q_LORA + MLA_KV_LORA + MLA_ROPE

GQA_Q_HEADS = 4
GQA_KV_HEADS = 2
GQA_GROUP = GQA_Q_HEADS // GQA_KV_HEADS
GQA_HEAD_DIM = 64
GQA_IN = (GQA_Q_HEADS + 2 * GQA_KV_HEADS) * GQA_HEAD_DIM

ML_HEADS = 4
ML_HEAD_DIM = 64
ML_WIDTH = ML_HEADS * ML_HEAD_DIM
ML_CHUNK = 64
ML_IN = 4 * ML_WIDTH + 4 * ML_HEADS

HY_WIDTH = 256
HY_ORDER = 2
HY_BANDS = 16
HY_FEAT = 1 + 2 * HY_BANDS
HY_HIDDEN = 64
HY_IN = (HY_ORDER + 1) * HY_WIDTH
HY_DECAY_TARGET = 1e-2
HY_SHORT_PCT = 0.3
HY_LONG_PCT = 1.5

N_BRANCH = 4
BRANCH_W = 256
GATE_IN = N_BRANCH * D_MODEL
IN_OFFSETS = (MLA_IN, MLA_IN + GQA_IN, MLA_IN + GQA_IN + ML_IN, MLA_IN + GQA_IN + ML_IN + HY_IN)

PEER_HEADS = 8
PEER_NKEYS = 128
PEER_DK = 256
PEER_TOPK = 16
PEER_BLOCK = 128

V7X_LANES = 128
V7X_VMEM_BYTES = 64 * 1024 * 1024

MXU_DTYPE = jnp.bfloat16

PEER_TN = 512
PEER_TI = 8
PEER_TE = PEER_TI * PEER_NKEYS

ML_LC = 256

ATT_TQ = 256
ATT_TK = 1280


def _flash_kernel(q_ref, kt_ref, v_ref, o_ref, s_ref, *, tk, nk):
    q = q_ref[0, 0]
    tq = q.shape[0]
    dv = v_ref.shape[-1]

    def scores(j, slot):
        start = pl.multiple_of(j * tk, tk)
        s_ref[slot] = jnp.dot(q, kt_ref[0, 0, :, pl.ds(start, tk)], preferred_element_type=jnp.float32)

    def update(j, slot, m, l, acc):
        s = s_ref[slot]
        m_new = jnp.maximum(m, jnp.max(s, axis=-1, keepdims=True))
        alpha = jnp.exp2(m - m_new)
        p = jnp.exp2(s - m_new)
        l = alpha * l + jnp.sum(p, axis=-1, keepdims=True)
        vv = v_ref[0, 0, pl.ds(pl.multiple_of(j * tk, tk), tk), :]
        acc = alpha * acc + jnp.dot(p.astype(vv.dtype), vv, preferred_element_type=jnp.float32)
        return m_new, l, acc

    def body(i, carry):
        m, l, acc = carry
        scores(2 * i + 1, 1)
        m, l, acc = update(2 * i, 0, m, l, acc)
        scores(2 * i + 2, 0)
        return update(2 * i + 1, 1, m, l, acc)

    m0 = jnp.full((tq, 1), -jnp.inf, jnp.float32)
    l0 = jnp.zeros((tq, 1), jnp.float32)
    acc0 = jnp.zeros((tq, dv), jnp.float32)
    scores(0, 0)
    m, l, acc = lax.fori_loop(0, (nk - 1) // 2, body, (m0, l0, acc0))
    _, l, acc = update(nk - 1, 0, m, l, acc)
    o_ref[0, 0] = (acc / l).astype(o_ref.dtype)


def flash_attention(q, kt, v, group):
    B, H, S, d = q.shape
    K = kt.shape[-1]
    dv = v.shape[-1]
    assert S % ATT_TQ == 0 and K % ATT_TK == 0
    nk = K // ATT_TK
    assert nk % 2 == 1
    return pl.pallas_call(
        functools.partial(_flash_kernel, tk=ATT_TK, nk=nk),
        grid=(B, H, S // ATT_TQ),
        in_specs=[
            pl.BlockSpec((1, 1, ATT_TQ, d), lambda b, h, i: (b, h, i, 0)),
            pl.BlockSpec((1, 1, d, K), lambda b, h, i: (b, h // group, 0, 0)),
            pl.BlockSpec((1, 1, K, dv), lambda b, h, i: (b, h // group, 0, 0)),
        ],
        out_specs=pl.BlockSpec((1, 1, ATT_TQ, dv), lambda b, h, i: (b, h, i, 0)),
        out_shape=jax.ShapeDtypeStruct((B, H, S, dv), jnp.float32),
        scratch_shapes=[pltpu.VMEM((2, ATT_TQ, ATT_TK), jnp.float32)],
        compiler_params=pltpu.CompilerParams(
            dimension_semantics=("arbitrary", "arbitrary", "arbitrary"),
            vmem_limit_bytes=V7X_VMEM_BYTES * 3 // 4),
        name="flash_attention",
    )(q, kt, v)


def latent_attention(q_lat, k_ctx, v_ctx, k_lat, v_lat, scale):
    B, S, Hk, G, dk = q_lat.shape
    k_all = jnp.concatenate([k_ctx, k_lat], axis=1)
    v_all = jnp.concatenate([v_ctx, v_lat], axis=1)
    q = (q_lat * (scale * math.log2(math.e))).astype(jnp.bfloat16).reshape(B, S, Hk * G, dk).transpose(0, 2, 1, 3)
    kt = k_all.astype(jnp.bfloat16).transpose(0, 2, 3, 1)
    v = v_all.astype(jnp.bfloat16).transpose(0, 2, 1, 3)
    o = flash_attention(q, kt, v, G)
    return o.transpose(0, 2, 1, 3).reshape(B, S, -1)


def rms_norm(x, g):
    xf = x.astype(jnp.float32)
    y = xf * lax.rsqrt(jnp.mean(xf * xf, axis=-1, keepdims=True) + EPS)
    return (y * g.astype(jnp.float32)).astype(x.dtype)


def modulate(x, g, shift, scale):
    return rms_norm(x, g) * (1.0 + scale) + shift


def axial_rope_tables(rows, cols, d_rot):
    m = d_rot // 2
    inv = ROPE_THETA ** (-jnp.arange(0, m, 2, dtype=jnp.float32) / m)
    ar = rows.astype(jnp.float32)[:, None] * inv
    ac = cols.astype(jnp.float32)[:, None] * inv
    return (jnp.cos(ar), jnp.sin(ar), jnp.cos(ac), jnp.sin(ac))


def _rotate(x, cos, sin):
    x1, x2 = jnp.split(x, 2, axis=-1)
    return jnp.concatenate([x1 * cos - x2 * sin, x2 * cos + x1 * sin], axis=-1)


def apply_axial_rope(x, tables):
    extra = x.ndim - 3
    t = [a.reshape((a.shape[0],) + (1,) * extra + (a.shape[1],)) for a in tables]
    xr, xc = jnp.split(x.astype(jnp.float32), 2, axis=-1)
    out = jnp.concatenate([_rotate(xr, t[0], t[1]), _rotate(xc, t[2], t[3])], axis=-1)
    return out.astype(x.dtype)


def attend(q, k, v, scale):
    s = jnp.einsum('bqhgd,bkhd->bhgqk', q, k, preferred_element_type=jnp.float32) * scale
    p = jax.nn.softmax(s, axis=-1).astype(v.dtype)
    return jnp.einsum('bhgqk,bkhd->bqhgd', p, v)


def mla_qkv(u, q_norm, kv_norm, w_uq, w_ukv, qk_norm_q, qk_norm_k, rope):
    B, L = u.shape[:2]
    c_q = u[..., :MLA_Q_LORA]
    c_kv = u[..., MLA_Q_LORA:MLA_Q_LORA + MLA_KV_LORA]
    k_pe = u[..., MLA_Q_LORA + MLA_KV_LORA:]
    q = (rms_norm(c_q, q_norm) @ w_uq).reshape(B, L, MLA_HEADS, MLA_QK)
    kv = (rms_norm(c_kv, kv_norm) @ w_ukv).reshape(B, L, MLA_HEADS, MLA_NOPE + MLA_V)
    k = jnp.concatenate([kv[..., :MLA_NOPE],
                         jnp.broadcast_to(k_pe[:, :, None, :], (B, L, MLA_HEADS, MLA_ROPE))], axis=-1)
    v = kv[..., MLA_NOPE:]
    q = rms_norm(q, qk_norm_q)
    k = rms_norm(k, qk_norm_k)
    if rope is not None:
        q = jnp.concatenate([q[..., :MLA_NOPE], apply_axial_rope(q[..., MLA_NOPE:], rope)], axis=-1)
        k = jnp.concatenate([k[..., :MLA_NOPE], apply_axial_rope(k[..., MLA_NOPE:], rope)], axis=-1)
    return q[:, :, :, None, :], k, v


def gqa_qkv(u, qk_norm_q, qk_norm_k, rope):
    B, L = u.shape[:2]
    nq = GQA_Q_HEADS * GQA_HEAD_DIM
    nk = GQA_KV_HEADS * GQA_HEAD_DIM
    q = rms_norm(u[..., :nq].reshape(B, L, GQA_KV_HEADS, GQA_GROUP, GQA_HEAD_DIM), qk_norm_q)
    k = rms_norm(u[..., nq:nq + nk].reshape(B, L, GQA_KV_HEADS, GQA_HEAD_DIM), qk_norm_k)
    v = u[..., nq + nk:].reshape(B, L, GQA_KV_HEADS, GQA_HEAD_DIM)
    if rope is not None:
        q = apply_axial_rope(q, rope)
        k = apply_axial_rope(k, rope)
    return q, k, v


def _mlstm_kernel(q_ref, k_ref, v_ref, gc_ref, gr_ref, c0_ref, n0_ref, m0_ref,
                  h_ref, c_ref, n_ref, m_ref):
    f32 = jnp.float32
    lc = q_ref.shape[1]
    fwd = pl.program_id(0) == 0

    @pl.when(pl.program_id(2) == 0)
    def _init():
        c_ref[...] = c0_ref[...]
        n_ref[...] = n0_ref[...]
        m_ref[...] = m0_ref[...]

    row = lax.broadcasted_iota(jnp.int32, (lc, lc), 0)
    col = lax.broadcasted_iota(jnp.int32, (lc, lc), 1)
    mask = jnp.where(fwd, row, col) >= jnp.where(fwd, col, row)
    tri = mask.astype(f32)
    gc = gc_ref[0, 0]
    gr = gr_ref[0, 0]
    b_cols = jnp.dot(tri, gc[:, :ML_HEADS], preferred_element_type=f32, precision=lax.Precision.HIGHEST)
    b_rows = lax.dot_general(gr[:ML_HEADS], tri, (((1,), (1,)), ((), ())), preferred_element_type=f32,
                             precision=lax.Precision.HIGHEST)
    for hh in range(ML_HEADS):
        sl = slice(hh * ML_HEAD_DIM, (hh + 1) * ML_HEAD_DIM)
        q = q_ref[0, :, sl]
        k = k_ref[0, :, sl] * (ML_HEAD_DIM ** -0.5)
        v = v_ref[0, :, sl]
        qb, vb = q.astype(MXU_DTYPE), v.astype(MXU_DTYPE)
        b_col = b_cols[:, hh:hh + 1]
        b_row = b_rows[hh:hh + 1, :]
        li_col = gc[:, ML_HEADS + hh:ML_HEADS + hh + 1]
        li_row = gr[ML_HEADS + hh:ML_HEADS + hh + 1, :]
        m = m_ref[0, 0, hh][:, :1]
        c_st = c_ref[0, 0, hh]
        n_st = n_ref[0, 0, hh]
        dmat = jnp.where(mask, b_col - b_row + li_row, -jnp.inf)
        inter = b_col + m
        m_t = jnp.maximum(inter, jnp.max(dmat, axis=1, keepdims=True))
        w_intra = jnp.exp(dmat - m_t)
        w_state = jnp.exp(inter - m_t)
        qk = lax.dot_general(qb, k.astype(MXU_DTYPE), (((1,), (1,)), ((), ())), preferred_element_type=f32)
        a = w_intra * qk
        num = (jnp.dot(a.astype(MXU_DTYPE), vb, preferred_element_type=f32)
               + w_state * jnp.dot(qb, c_st.astype(MXU_DTYPE), preferred_element_type=f32))
        den = jnp.sum(a, axis=1, keepdims=True) + w_state * jnp.sum(q * n_st, axis=1, keepdims=True)
        h_ref[0, 0, :, sl] = num / jnp.maximum(jnp.abs(den), jnp.exp(-m_t))
        b_last = jnp.where(fwd, b_col[lc - 1:lc], b_col[0:1])
        g = b_last - b_col + li_col
        m_new = jnp.maximum(b_last + m, jnp.max(g, axis=0, keepdims=True))
        w_c = jnp.exp(b_last + m - m_new)
        kw = k * jnp.exp(g - m_new)
        c_ref[0, 0, hh] = w_c * c_st + lax.dot_general(kw.astype(MXU_DTYPE), vb, (((0,), (0,)), ((), ())),
                                                       preferred_element_type=f32)
        n_ref[0, 0, hh] = w_c * n_st + jnp.sum(kw, axis=0, keepdims=True)
        m_ref[0, 0, hh] = jnp.broadcast_to(m_new, (1, V7X_LANES))


def bidir_mlstm(u, gate_b, state):
    B, L = u.shape[:2]
    f32 = jnp.float32
    lc = min(ML_LC, L)
    assert L % lc == 0
    nc = L // lc
    w = ML_WIDTH
    gates = u[..., 4 * w:].astype(f32).reshape(B, L, 4, ML_HEADS) + gate_b.astype(f32)
    gcol = jnp.stack([jnp.concatenate([jax.nn.log_sigmoid(gates[:, :, 2]), gates[:, :, 0]], axis=-1),
                      jnp.concatenate([jax.nn.log_sigmoid(gates[:, :, 3]), gates[:, :, 1]], axis=-1)])
    grow = gcol.transpose(0, 1, 3, 2)
    chunk = lambda d, c: c + d * (nc - 1 - 2 * c)
    seq_blk = lambda j: pl.BlockSpec((1, lc, w), lambda d, b, c: (b, chunk(d, c), j))
    st_blk = lambda shape: pl.BlockSpec((1, 1) + shape, lambda d, b, c: (d, b) + (0,) * len(shape))
    c_sh, n_sh, m_sh = (ML_HEADS, ML_HEAD_DIM, ML_HEAD_DIM), (ML_HEADS, 1, ML_HEAD_DIM), (ML_HEADS, 1, V7X_LANES)
    h, c_st, n_st, m_st = pl.pallas_call(
        _mlstm_kernel,
        grid=(2, B, nc),
        in_specs=[seq_blk(0), seq_blk(1), seq_blk(2),
                  pl.BlockSpec((1, 1, lc, 2 * ML_HEADS), lambda d, b, c: (d, b, chunk(d, c), 0)),
                  pl.BlockSpec((1, 1, 2 * ML_HEADS, lc), lambda d, b, c: (d, b, 0, chunk(d, c))),
                  st_blk(c_sh), st_blk(n_sh), st_blk(m_sh)],
        out_specs=[pl.BlockSpec((1, 1, lc, w), lambda d, b, c: (d, b, chunk(d, c), 0)),
                   st_blk(c_sh), st_blk(n_sh), st_blk(m_sh)],
        out_shape=[jax.ShapeDtypeStruct((2, B, L, w), f32),
                   jax.ShapeDtypeStruct((2, B) + c_sh, f32),
                   jax.ShapeDtypeStruct((2, B) + n_sh, f32),
                   jax.ShapeDtypeStruct((2, B) + m_sh, f32)],
        compiler_params=pltpu.CompilerParams(
            dimension_semantics=("arbitrary", "arbitrary", "arbitrary"),
            vmem_limit_bytes=V7X_VMEM_BYTES * 3 // 4),
        name="mlstm",
    )(u, u, u, gcol, grow, *state)
    return h[0] + h[1], (c_st, n_st, m_st)


def mlstm_mixer(u_ctx, u_lat, gate_b, norm_g, with_ctx):
    B = u_lat.shape[0]
    f32 = jnp.float32
    zero = (jnp.zeros((2, B, ML_HEADS, ML_HEAD_DIM, ML_HEAD_DIM), f32),
            jnp.zeros((2, B, ML_HEADS, 1, ML_HEAD_DIM), f32),
            jnp.zeros((2, B, ML_HEADS, 1, V7X_LANES), f32))
    h_c, st = bidir_mlstm(u_ctx, gate_b, zero)
    h_l, _ = bidir_mlstm(u_lat, gate_b, st)

    def out(h, u):
        Bq, L = u.shape[:2]
        o = u[..., 3 * ML_WIDTH:4 * ML_WIDTH]
        hn = rms_norm(h.reshape(Bq, L, ML_HEADS, ML_HEAD_DIM), norm_g).reshape(Bq, L, ML_WIDTH)
        return (jax.nn.sigmoid(o.astype(f32)) * hn).astype(u.dtype)

    return (out(h_c, u_ctx) if with_ctx else None), out(h_l, u_lat)


def hyena_filters(L, w1, b1, freq, w2, b2, w3):
    f32 = jnp.float32
    t = jnp.arange(L, dtype=f32) / L
    ang = 2.0 * math.pi * t[:, None] * jnp.arange(1, HY_BANDS + 1, dtype=f32)
    z = jnp.concatenate([t[:, None], jnp.sin(ang), jnp.cos(ang)], axis=-1)
    hdn = jnp.sin(freq[0].astype(f32) * (z @ w1.astype(f32) + b1.astype(f32)))
    hdn = jnp.sin(freq[1].astype(f32) * (hdn @ w2.astype(f32) + b2.astype(f32)))
    filt = (hdn @ w3.astype(f32)).reshape(L, HY_ORDER, 2, HY_WIDTH)
    log_target = math.log(HY_DECAY_TARGET)
    alpha = jnp.linspace(-log_target / HY_LONG_PCT, -log_target / HY_SHORT_PCT, HY_WIDTH, dtype=f32)
    filt = filt * jnp.exp(-t[:, None] * alpha)[:, None, None, :]
    return filt * lax.rsqrt(jnp.sum(filt * filt, axis=(0, 2), keepdims=True) + EPS)


def two_sided_fftconv(z, h_fwd, h_bwd):
    L, C = h_fwd.shape
    h_circ = jnp.concatenate([h_fwd, jnp.zeros((1, C), h_fwd.dtype), h_bwd[:0:-1]], axis=0)
    hf = jnp.fft.rfft(h_circ, n=2 * L, axis=0)
    zf = jnp.fft.rfft(z.astype(jnp.float32), n=2 * L, axis=1)
    y = jnp.fft.irfft(zf * hf[None], n=2 * L, axis=1)[:, :L]
    return y.astype(z.dtype)


def short_conv3(u, w):
    up = jnp.pad(u, ((0, 0), (1, 1), (0, 0)))
    return up[:, :-2] * w[0] + up[:, 1:-1] * w[1] + up[:, 2:] * w[2]


def hyena_mixer(u, conv_w, filt, bias):
    u = short_conv3(u, conv_w)
    v, x1, x2 = jnp.split(u, 3, axis=-1)
    z = v
    for order, gate in enumerate((x1, x2)):
        z = gate * (two_sided_fftconv(z, filt[:, order, 0], filt[:, order, 1]) + z * bias[order])
    return z


def merge_branches(ys, gate_logits, w_branch, w_out):
    B, L = gate_logits.shape[:2]
    gl = gate_logits.reshape(B, L, N_BRANCH, -1)
    acc = jax.nn.sigmoid(gl[:, :, 0]) * (ys[0] @ w_branch[0])
    for i in range(1, N_BRANCH):
        acc = acc + jax.nn.sigmoid(gl[:, :, i]) * (ys[i] @ w_branch[i])
    return acc @ w_out


def _topk_rows(s, k):
    rows = []
    for _ in range(k):
        m = jnp.max(s, axis=0, keepdims=True)
        rows.append(m)
        s = jnp.where(s == m, -jnp.inf, s)
    return rows


def _peer_route_kernel(ht_ref, wqt_ref, sk_ref, s1_ref, s2_ref, ea_ref, eb_ref, thr_ref):
    f32 = jnp.float32
    tn = ht_ref.shape[1]
    half = PEER_DK // 2
    qt = jnp.dot(wqt_ref[...], ht_ref[...], preferred_element_type=f32).astype(sk_ref.dtype)
    s1 = jnp.dot(sk_ref[0, 0], qt[:half], preferred_element_type=f32)
    s2 = jnp.dot(sk_ref[0, 1], qt[half:], preferred_element_type=f32)
    sv1 = _topk_rows(s1, PEER_TOPK)
    sv2 = _topk_rows(s2, PEER_TOPK)
    sv2_stack = jnp.concatenate(sv2, axis=0)
    row = lax.broadcasted_iota(jnp.int32, (8, tn), 0)
    tiles = [sv1[0] + sv2_stack]
    for a in range(1, PEER_TOPK):
        nb = PEER_TOPK // (a + 1)
        t = sv1[a] + sv2_stack[:8]
        if nb < 8:
            t = jnp.where(row < nb, t, -jnp.inf)
        tiles.append(t)
    cand = jnp.concatenate(tiles, axis=0)
    cmax = sv1[0] + sv2[0]
    thr = _topk_rows(cand, PEER_TOPK)[-1]
    z = jnp.sum(jnp.where(cand >= thr, jnp.exp(cand - cmax), 0.0), axis=0, keepdims=True)
    s1_ref[0] = s1
    s2_ref[0] = s2
    ea_ref[0] = jnp.exp(s1 - sv1[0]) / z
    eb_ref[0] = jnp.exp(s2 - sv2[0])
    thr_ref[0] = thr


def _gelu_tanh(x):
    return x * (0.5 * (1.0 + jnp.tanh(0.7978845608028654 * (x + 0.044715 * (x * x * x)))))


def _peer_expert_kernel(ht_ref, u_ref, vt_ref, s1_ref, s2_ref, ea_ref, eb_ref, thr_ref, o_ref, w_ref):
    f32 = jnp.float32
    e = pl.program_id(1)
    tn = ht_ref.shape[1]

    @pl.when(e == 0)
    def _init():
        o_ref[...] = jnp.zeros_like(o_ref)

    for ii in range(PEER_TI):
        rows = slice(ii * PEER_NKEYS, (ii + 1) * PEER_NKEYS)
        for tc in range(tn // V7X_LANES):
            cols = slice(tc * V7X_LANES, (tc + 1) * V7X_LANES)
            w = jnp.zeros((PEER_NKEYS, V7X_LANES), f32)
            for h in range(PEER_HEADS):
                s = s2_ref[h, :, cols] + s1_ref[h, ii:ii + 1, cols]
                w = w + jnp.where(s >= thr_ref[h, :, cols], eb_ref[h, :, cols], 0.0) * ea_ref[h, ii:ii + 1, cols]
            w_ref[rows, cols] = w
    a = jnp.dot(u_ref[...], ht_ref[...], preferred_element_type=f32)
    wa = (w_ref[...] * _gelu_tanh(a)).astype(vt_ref.dtype)
    o_ref[...] += jnp.dot(vt_ref[...], wa, preferred_element_type=f32)


def peer(h, wq, subkeys, u_tab, v_tab):
    T, D = h.shape
    assert T % PEER_TN == 0
    nt = T // PEER_TN
    ne = PEER_NKEYS // PEER_TI
    f32 = jnp.float32
    ht = h.T.astype(MXU_DTYPE)
    wqt = wq.T.astype(MXU_DTYPE)
    sk = subkeys.astype(MXU_DTYPE)
    u = u_tab.astype(MXU_DTYPE)
    vt = v_tab.T.astype(MXU_DTYPE)
    head_blk = pl.BlockSpec((1, PEER_NKEYS, PEER_TN), lambda n, hh: (hh, 0, n))
    head_shape = jax.ShapeDtypeStruct((PEER_HEADS, PEER_NKEYS, T), f32)
    s1, s2, ea, eb, thr = pl.pallas_call(
        _peer_route_kernel,
        grid=(nt, PEER_HEADS),
        in_specs=[
            pl.BlockSpec((D, PEER_TN), lambda n, hh: (0, n)),
            pl.BlockSpec((PEER_DK, D), lambda n, hh: (hh, 0)),
            pl.BlockSpec((1, 2, PEER_NKEYS, PEER_DK // 2), lambda n, hh: (hh, 0, 0, 0)),
        ],
        out_specs=[head_blk, head_blk, head_blk, head_blk,
                   pl.BlockSpec((1, 1, PEER_TN), lambda n, hh: (hh, 0, n))],
        out_shape=[head_shape, head_shape, head_shape, head_shape,
                   jax.ShapeDtypeStruct((PEER_HEADS, 1, T), f32)],
        compiler_params=pltpu.CompilerParams(
            dimension_semantics=("arbitrary", "arbitrary"),
            vmem_limit_bytes=V7X_VMEM_BYTES * 3 // 4),
        name="peer_route",
    )(ht, wqt, sk)

    tok_blk = pl.BlockSpec((PEER_HEADS, PEER_NKEYS, PEER_TN), lambda n, e: (0, 0, n))
    row_blk = pl.BlockSpec((PEER_HEADS, PEER_TI, PEER_TN), lambda n, e: (0, e, n))
    out_t = pl.pallas_call(
        _peer_expert_kernel,
        grid=(nt, ne),
        in_specs=[
            pl.BlockSpec((D, PEER_TN), lambda n, e: (0, n)),
            pl.BlockSpec((PEER_TE, D), lambda n, e: (e, 0)),
            pl.BlockSpec((D, PEER_TE), lambda n, e: (0, e)),
            row_blk, tok_blk, row_blk, tok_blk,
            pl.BlockSpec((PEER_HEADS, 1, PEER_TN), lambda n, e: (0, 0, n)),
        ],
        out_specs=pl.BlockSpec((D, PEER_TN), lambda n, e: (0, n)),
        out_shape=jax.ShapeDtypeStruct((D, T), f32),
        scratch_shapes=[pltpu.VMEM((PEER_TE, PEER_TN), f32)],
        compiler_params=pltpu.CompilerParams(
            dimension_semantics=("arbitrary", "arbitrary"),
            vmem_limit_bytes=V7X_VMEM_BYTES * 3 // 4),
        name="peer_expert",
    )(ht, u, vt, s1, s2, ea, eb, thr)
    return out_t.T.astype(h.dtype)


def trunk_layer(xc, xl, mod_c, mod_l, rope_a, rope_b, with_ctx, norm1, norm2, w_in,
                mla_q_norm, mla_kv_norm, mla_w_uq, mla_w_ukv, mla_qk_norm_q, mla_qk_norm_k,
                gqa_qk_norm_q, gqa_qk_norm_k, ml_gate_b, ml_norm,
                hy_conv, hy_w1, hy_b1, hy_freq, hy_w2, hy_b2, hy_w3, hy_bias,
                w_branch, w_out, peer_wq, peer_subkeys, peer_u, peer_v):
    D = xl.shape[-1]
    sh1_c, sc1_c, g1_c, sh2_c, sc2_c, g2_c = jnp.split(mod_c, 6, axis=-1)
    sh1_l, sc1_l, g1_l, sh2_l, sc2_l, g2_l = [a[:, None, :] for a in jnp.split(mod_l, 6, axis=-1)]

    pc = modulate(xc, norm1, sh1_c, sc1_c) @ w_in
    pl_ = modulate(xl, norm1, sh1_l, sc1_l) @ w_in
    a_c, b_c, m_c, h_c, gate_c = jnp.split(pc, IN_OFFSETS, axis=-1)
    a_l, b_l, m_l, h_l, gate_l = jnp.split(pl_, IN_OFFSETS, axis=-1)

    qa_c, ka_c, va_c = mla_qkv(a_c, mla_q_norm, mla_kv_norm, mla_w_uq, mla_w_ukv, mla_qk_norm_q, mla_qk_norm_k, None)
    qa_l, ka_l, va_l = mla_qkv(a_l, mla_q_norm, mla_kv_norm, mla_w_uq, mla_w_ukv, mla_qk_norm_q, mla_qk_norm_k, rope_a)
    ya_l = latent_attention(qa_l, ka_c, va_c, ka_l, va_l, MLA_QK ** -0.5)
    qb_c, kb_c, vb_c = gqa_qkv(b_c, gqa_qk_norm_q, gqa_qk_norm_k, None)
    qb_l, kb_l, vb_l = gqa_qkv(b_l, gqa_qk_norm_q, gqa_qk_norm_k, rope_b)
    yb_l = latent_attention(qb_l, kb_c, vb_c, kb_l, vb_l, GQA_HEAD_DIM ** -0.5)
    yc_c, yc_l = mlstm_mixer(m_c, m_l, ml_gate_b, ml_norm, with_ctx)
    filt_l = hyena_filters(xl.shape[1], hy_w1, hy_b1, hy_freq, hy_w2, hy_b2, hy_w3)
    yd_l = hyena_mixer(h_l, hy_conv, filt_l, hy_bias)

    xl = xl + g1_l * merge_branches([ya_l, yb_l, yc_l, yd_l], gate_l, w_branch, w_out)
    if with_ctx:
        Bc, Lc = xc.shape[:2]
        ya_c = attend(qa_c, ka_c, va_c, MLA_QK ** -0.5).reshape(Bc, Lc, -1)
        yb_c = attend(qb_c, kb_c, vb_c, GQA_HEAD_DIM ** -0.5).reshape(Bc, Lc, -1)
        filt_c = hyena_filters(Lc, hy_w1, hy_b1, hy_freq, hy_w2, hy_b2, hy_w3)
        yd_c = hyena_mixer(h_c, hy_conv, filt_c, hy_bias)
        xc = xc + g1_c * merge_branches([ya_c, yb_c, yc_c, yd_c], gate_c, w_branch, w_out)

    h2l = modulate(xl, norm2, sh2_l, sc2_l)
    if with_ctx:
        h2c = modulate(xc, norm2, sh2_c, sc2_c)
        n_ctx = xc.shape[0] * xc.shape[1]
        f = peer(jnp.concatenate([h2c.reshape(-1, D), h2l.reshape(-1, D)], axis=0), peer_wq, peer_subkeys, peer_u, peer_v)
        xc = xc + g2_c * f[:n_ctx].reshape(xc.shape)
        xl = xl + g2_l * f[n_ctx:].reshape(xl.shape)
    else:
        xl = xl + g2_l * peer(h2l.reshape(-1, D), peer_wq, peer_subkeys, peer_u, peer_v).reshape(xl.shape)
    return xc, xl


def kernel(x, c, ctx, c_ctx, ada_w, ada_b, norm1, norm2, w_in,
           mla_q_norm, mla_kv_norm, mla_w_uq, mla_w_ukv, mla_qk_norm_q, mla_qk_norm_k,
           gqa_qk_norm_q, gqa_qk_norm_k, ml_gate_b, ml_norm,
           hy_conv, hy_w1, hy_b1, hy_freq, hy_w2, hy_b2, hy_w3, hy_bias,
           w_branch, w_out, peer_wq, peer_subkeys, peer_u, peer_v):
    B, S, D = x.shape
    ROWS = S // GRID_W
    rows = jnp.broadcast_to(jnp.arange(ROWS, dtype=jnp.int32)[:, None], (ROWS, GRID_W)).reshape(-1)
    cols = jnp.broadcast_to(jnp.arange(GRID_W, dtype=jnp.int32)[None, :], (ROWS, GRID_W)).reshape(-1)
    rope_a = axial_rope_tables(rows, cols, MLA_ROPE)
    rope_b = axial_rope_tables(rows, cols, GQA_HEAD_DIM)
    s_lat = jax.nn.silu(c)
    s_ctx = jax.nn.silu(c_ctx)
    xc, xl = ctx, x
    for l in range(DEPTH):
        mod_l = s_lat @ ada_w[l] + ada_b[l]
        mod_c = s_ctx @ ada_w[l] + ada_b[l]
        xc, xl = trunk_layer(
            xc, xl, mod_c, mod_l, rope_a, rope_b, l < DEPTH - 1, norm1[l], norm2[l], w_in[l],
            mla_q_norm[l], mla_kv_norm[l], mla_w_uq[l], mla_w_ukv[l], mla_qk_norm_q[l], mla_qk_norm_k[l],
            gqa_qk_norm_q[l], gqa_qk_norm_k[l], ml_gate_b[l], ml_norm[l],
            hy_conv[l], hy_w1[l], hy_b1[l], hy_freq[l], hy_w2[l], hy_b2[l], hy_w3[l], hy_bias[l],
            w_branch[l], w_out[l], peer_wq[l], peer_subkeys[l], peer_u[l], peer_v[l])
    return xl
```

```python
import functools
import itertools
import math

import jax
import jax.numpy as jnp
from jax import lax
from jax.experimental import pallas as pl
from jax.experimental.pallas import tpu as pltpu

D_MODEL = 1024
DEPTH = 4
GRID_W = 64
ROPE_THETA = 10000.0
EPS = 1e-6

MLA_HEADS = 4
MLA_Q_LORA = 256
MLA_KV_LORA = 128
MLA_NOPE = 64
MLA_ROPE = 32
MLA_V = 64
MLA_QK = MLA_NOPE + MLA_ROPE
MLA_IN = MLA_Q_LORA + MLA_KV_LORA + MLA_ROPE

GQA_Q_HEADS = 4
GQA_KV_HEADS = 2
GQA_GROUP = GQA_Q_HEADS // GQA_KV_HEADS
GQA_HEAD_DIM = 64
GQA_IN = (GQA_Q_HEADS + 2 * GQA_KV_HEADS) * GQA_HEAD_DIM

ML_HEADS = 4
ML_HEAD_DIM = 64
ML_WIDTH = ML_HEADS * ML_HEAD_DIM
ML_CHUNK = 64
ML_IN = 4 * ML_WIDTH + 4 * ML_HEADS

HY_WIDTH = 256
HY_ORDER = 2
HY_BANDS = 16
HY_FEAT = 1 + 2 * HY_BANDS
HY_HIDDEN = 64
HY_IN = (HY_ORDER + 1) * HY_WIDTH
HY_DECAY_TARGET = 1e-2
HY_SHORT_PCT = 0.3
HY_LONG_PCT = 1.5

N_BRANCH = 4
BRANCH_W = 256
GATE_IN = N_BRANCH * D_MODEL
IN_OFFSETS = (MLA_IN, MLA_IN + GQA_IN, MLA_IN + GQA_IN + ML_IN, MLA_IN + GQA_IN + ML_IN + HY_IN)

PEER_HEADS = 8
PEER_NKEYS = 128
PEER_DK = 256
PEER_TOPK = 16
PEER_BLOCK = 128

V7X_LANES = 128
V7X_VMEM_BYTES = 64 * 1024 * 1024

MXU_DTYPE = jnp.bfloat16

PEER_TN = 512
PEER_TI = 8
PEER_TE = PEER_TI * PEER_NKEYS
PEER_SUB = 512
PEER_JC = 32

ML_LC = 256

ATT_TQ = 256
ATT_TK = 1280


def _flash_kernel(q_ref, kt_ref, v_ref, o_ref, s_ref, *, tk, nk):
    q = q_ref[0, 0]
    tq = q.shape[0]
    dv = v_ref.shape[-1]

    def scores(j, slot):
        start = pl.multiple_of(j * tk, tk)
        s_ref[slot] = jnp.dot(q, kt_ref[0, 0, :, pl.ds(start, tk)], preferred_element_type=jnp.float32)

    def update(j, slot, m, l, acc):
        s = s_ref[slot]
        m_new = jnp.maximum(m, jnp.max(s, axis=-1, keepdims=True))
        alpha = jnp.exp2(m - m_new)
        p = jnp.exp2(s - m_new)
        l = alpha * l + jnp.sum(p, axis=-1, keepdims=True)
        vv = v_ref[0, 0, pl.ds(pl.multiple_of(j * tk, tk), tk), :]
        acc = alpha * acc + jnp.dot(p.astype(vv.dtype), vv, preferred_element_type=jnp.float32)
        return m_new, l, acc

    def body(i, carry):
        m, l, acc = carry
        scores(2 * i + 1, 1)
        m, l, acc = update(2 * i, 0, m, l, acc)
        scores(2 * i + 2, 0)
        return update(2 * i + 1, 1, m, l, acc)

    m0 = jnp.full((tq, 1), -jnp.inf, jnp.float32)
    l0 = jnp.zeros((tq, 1), jnp.float32)
    acc0 = jnp.zeros((tq, dv), jnp.float32)
    scores(0, 0)
    m, l, acc = lax.fori_loop(0, (nk - 1) // 2, body, (m0, l0, acc0))
    _, l, acc = update(nk - 1, 0, m, l, acc)
    o_ref[0, 0] = (acc / l).astype(o_ref.dtype)


def flash_attention(q, kt, v, group):
    B, H, S, d = q.shape
    K = kt.shape[-1]
    dv = v.shape[-1]
    assert S % ATT_TQ == 0 and K % ATT_TK == 0
    nk = K // ATT_TK
    assert nk % 2 == 1
    return pl.pallas_call(
        functools.partial(_flash_kernel, tk=ATT_TK, nk=nk),
        grid=(B, H, S // ATT_TQ),
        in_specs=[
            pl.BlockSpec((1, 1, ATT_TQ, d), lambda b, h, i: (b, h, i, 0)),
            pl.BlockSpec((1, 1, d, K), lambda b, h, i: (b, h // group, 0, 0)),
            pl.BlockSpec((1, 1, K, dv), lambda b, h, i: (b, h // group, 0, 0)),
        ],
        out_specs=pl.BlockSpec((1, 1, ATT_TQ, dv), lambda b, h, i: (b, h, i, 0)),
        out_shape=jax.ShapeDtypeStruct((B, H, S, dv), jnp.float32),
        scratch_shapes=[pltpu.VMEM((2, ATT_TQ, ATT_TK), jnp.float32)],
        compiler_params=pltpu.CompilerParams(
            dimension_semantics=("arbitrary", "arbitrary", "arbitrary"),
            vmem_limit_bytes=V7X_VMEM_BYTES * 3 // 4),
        name="flash_attention",
    )(q, kt, v)


def latent_attention(q_lat, k_ctx, v_ctx, k_lat, v_lat, scale):
    B, S, Hk, G, dk = q_lat.shape
    k_all = jnp.concatenate([k_ctx, k_lat], axis=1)
    v_all = jnp.concatenate([v_ctx, v_lat], axis=1)
    q = (q_lat * (scale * math.log2(math.e))).astype(jnp.bfloat16).reshape(B, S, Hk * G, dk).transpose(0, 2, 1, 3)
    kt = k_all.astype(jnp.bfloat16).transpose(0, 2, 3, 1)
    v = v_all.astype(jnp.bfloat16).transpose(0, 2, 1, 3)
    o = flash_attention(q, kt, v, G)
    return o.transpose(0, 2, 1, 3).reshape(B, S, -1)


def rms_norm(x, g):
    xf = x.astype(jnp.float32)
    y = xf * lax.rsqrt(jnp.mean(xf * xf, axis=-1, keepdims=True) + EPS)
    return (y * g.astype(jnp.float32)).astype(x.dtype)


def modulate(x, g, shift, scale):
    return rms_norm(x, g) * (1.0 + scale) + shift


def axial_rope_tables(rows, cols, d_rot):
    m = d_rot // 2
    inv = ROPE_THETA ** (-jnp.arange(0, m, 2, dtype=jnp.float32) / m)
    ar = rows.astype(jnp.float32)[:, None] * inv
    ac = cols.astype(jnp.float32)[:, None] * inv
    return (jnp.cos(ar), jnp.sin(ar), jnp.cos(ac), jnp.sin(ac))


def _rotate(x, cos, sin):
    x1, x2 = jnp.split(x, 2, axis=-1)
    return jnp.concatenate([x1 * cos - x2 * sin, x2 * cos + x1 * sin], axis=-1)


def apply_axial_rope(x, tables):
    extra = x.ndim - 3
    t = [a.reshape((a.shape[0],) + (1,) * extra + (a.shape[1],)) for a in tables]
    xr, xc = jnp.split(x.astype(jnp.float32), 2, axis=-1)
    out = jnp.concatenate([_rotate(xr, t[0], t[1]), _rotate(xc, t[2], t[3])], axis=-1)
    return out.astype(x.dtype)


def attend(q, k, v, scale):
    s = jnp.einsum('bqhgd,bkhd->bhgqk', q, k, preferred_element_type=jnp.float32) * scale
    p = jax.nn.softmax(s, axis=-1).astype(v.dtype)
    return jnp.einsum('bhgqk,bkhd->bqhgd', p, v)


def mla_qkv(u, q_norm, kv_norm, w_uq, w_ukv, qk_norm_q, qk_norm_k, rope):
    B, L = u.shape[:2]
    c_q = u[..., :MLA_Q_LORA]
    c_kv = u[..., MLA_Q_LORA:MLA_Q_LORA + MLA_KV_LORA]
    k_pe = u[..., MLA_Q_LORA + MLA_KV_LORA:]
    q = (rms_norm(c_q, q_norm) @ w_uq).reshape(B, L, MLA_HEADS, MLA_QK)
    kv = (rms_norm(c_kv, kv_norm) @ w_ukv).reshape(B, L, MLA_HEADS, MLA_NOPE + MLA_V)
    k = jnp.concatenate([kv[..., :MLA_NOPE],
                         jnp.broadcast_to(k_pe[:, :, None, :], (B, L, MLA_HEADS, MLA_ROPE))], axis=-1)
    v = kv[..., MLA_NOPE:]
    q = rms_norm(q, qk_norm_q)
    k = rms_norm(k, qk_norm_k)
    if rope is not None:
        q = jnp.concatenate([q[..., :MLA_NOPE], apply_axial_rope(q[..., MLA_NOPE:], rope)], axis=-1)
        k = jnp.concatenate([k[..., :MLA_NOPE], apply_axial_rope(k[..., MLA_NOPE:], rope)], axis=-1)
    return q[:, :, :, None, :], k, v


def gqa_qkv(u, qk_norm_q, qk_norm_k, rope):
    B, L = u.shape[:2]
    nq = GQA_Q_HEADS * GQA_HEAD_DIM
    nk = GQA_KV_HEADS * GQA_HEAD_DIM
    q = rms_norm(u[..., :nq].reshape(B, L, GQA_KV_HEADS, GQA_GROUP, GQA_HEAD_DIM), qk_norm_q)
    k = rms_norm(u[..., nq:nq + nk].reshape(B, L, GQA_KV_HEADS, GQA_HEAD_DIM), qk_norm_k)
    v = u[..., nq + nk:].reshape(B, L, GQA_KV_HEADS, GQA_HEAD_DIM)
    if rope is not None:
        q = apply_axial_rope(q, rope)
        k = apply_axial_rope(k, rope)
    return q, k, v


def _mlstm_kernel(q_ref, k_ref, v_ref, gc_ref, gr_ref, c0_ref, n0_ref, m0_ref,
                  h_ref, c_ref, n_ref, m_ref):
    f32 = jnp.float32
    lc = q_ref.shape[1]
    fwd = pl.program_id(0) == 0

    @pl.when(pl.program_id(2) == 0)
    def _init():
        c_ref[...] = c0_ref[...]
        n_ref[...] = n0_ref[...]
        m_ref[...] = m0_ref[...]

    row = lax.broadcasted_iota(jnp.int32, (lc, lc), 0)
    col = lax.broadcasted_iota(jnp.int32, (lc, lc), 1)
    mask = jnp.where(fwd, row, col) >= jnp.where(fwd, col, row)
    tri = mask.astype(f32)
    gc = gc_ref[0, 0]
    gr = gr_ref[0, 0]
    b_cols = jnp.dot(tri, gc[:, :ML_HEADS], preferred_element_type=f32, precision=lax.Precision.HIGHEST)
    b_rows = lax.dot_general(gr[:ML_HEADS], tri, (((1,), (1,)), ((), ())), preferred_element_type=f32,
                             precision=lax.Precision.HIGHEST)
    for hh in range(ML_HEADS):
        sl = slice(hh * ML_HEAD_DIM, (hh + 1) * ML_HEAD_DIM)
        q = q_ref[0, :, sl]
        k = k_ref[0, :, sl] * (ML_HEAD_DIM ** -0.5)
        v = v_ref[0, :, sl]
        qb, vb = q.astype(MXU_DTYPE), v.astype(MXU_DTYPE)
        b_col = b_cols[:, hh:hh + 1]
        b_row = b_rows[hh:hh + 1, :]
        li_col = gc[:, ML_HEADS + hh:ML_HEADS + hh + 1]
        li_row = gr[ML_HEADS + hh:ML_HEADS + hh + 1, :]
        m = m_ref[0, 0, hh][:, :1]
        c_st = c_ref[0, 0, hh]
        n_st = n_ref[0, 0, hh]
        dmat = jnp.where(mask, b_col - b_row + li_row, -jnp.inf)
        inter = b_col + m
        m_t = jnp.maximum(inter, jnp.max(dmat, axis=1, keepdims=True))
        w_intra = jnp.exp(dmat - m_t)
        w_state = jnp.exp(inter - m_t)
        qk = lax.dot_general(qb, k.astype(MXU_DTYPE), (((1,), (1,)), ((), ())), preferred_element_type=f32)
        a = w_intra * qk
        num = (jnp.dot(a.astype(MXU_DTYPE), vb, preferred_element_type=f32)
               + w_state * jnp.dot(qb, c_st.astype(MXU_DTYPE), preferred_element_type=f32))
        den = jnp.sum(a, axis=1, keepdims=True) + w_state * jnp.sum(q * n_st, axis=1, keepdims=True)
        h_ref[0, 0, :, sl] = num / jnp.maximum(jnp.abs(den), jnp.exp(-m_t))
        b_last = jnp.where(fwd, b_col[lc - 1:lc], b_col[0:1])
        g = b_last - b_col + li_col
        m_new = jnp.maximum(b_last + m, jnp.max(g, axis=0, keepdims=True))
        w_c = jnp.exp(b_last + m - m_new)
        kw = k * jnp.exp(g - m_new)
        c_ref[0, 0, hh] = w_c * c_st + lax.dot_general(kw.astype(MXU_DTYPE), vb, (((0,), (0,)), ((), ())),
                                                       preferred_element_type=f32)
        n_ref[0, 0, hh] = w_c * n_st + jnp.sum(kw, axis=0, keepdims=True)
        m_ref[0, 0, hh] = jnp.broadcast_to(m_new, (1, V7X_LANES))


def bidir_mlstm(u, gate_b, state):
    B, L = u.shape[:2]
    f32 = jnp.float32
    lc = min(ML_LC, L)
    assert L % lc == 0
    nc = L // lc
    w = ML_WIDTH
    gates = u[..., 4 * w:].astype(f32).reshape(B, L, 4, ML_HEADS) + gate_b.astype(f32)
    gcol = jnp.stack([jnp.concatenate([jax.nn.log_sigmoid(gates[:, :, 2]), gates[:, :, 0]], axis=-1),
                      jnp.concatenate([jax.nn.log_sigmoid(gates[:, :, 3]), gates[:, :, 1]], axis=-1)])
    grow = gcol.transpose(0, 1, 3, 2)
    chunk = lambda d, c: c + d * (nc - 1 - 2 * c)
    seq_blk = lambda j: pl.BlockSpec((1, lc, w), lambda d, b, c: (b, chunk(d, c), j))
    st_blk = lambda shape: pl.BlockSpec((1, 1) + shape, lambda d, b, c: (d, b) + (0,) * len(shape))
    c_sh, n_sh, m_sh = (ML_HEADS, ML_HEAD_DIM, ML_HEAD_DIM), (ML_HEADS, 1, ML_HEAD_DIM), (ML_HEADS, 1, V7X_LANES)
    h, c_st, n_st, m_st = pl.pallas_call(
        _mlstm_kernel,
        grid=(2, B, nc),
        in_specs=[seq_blk(0), seq_blk(1), seq_blk(2),
                  pl.BlockSpec((1, 1, lc, 2 * ML_HEADS), lambda d, b, c: (d, b, chunk(d, c), 0)),
                  pl.BlockSpec((1, 1, 2 * ML_HEADS, lc), lambda d, b, c: (d, b, 0, chunk(d, c))),
                  st_blk(c_sh), st_blk(n_sh), st_blk(m_sh)],
        out_specs=[pl.BlockSpec((1, 1, lc, w), lambda d, b, c: (d, b, chunk(d, c), 0)),
                   st_blk(c_sh), st_blk(n_sh), st_blk(m_sh)],
        out_shape=[jax.ShapeDtypeStruct((2, B, L, w), f32),
                   jax.ShapeDtypeStruct((2, B) + c_sh, f32),
                   jax.ShapeDtypeStruct((2, B) + n_sh, f32),
                   jax.ShapeDtypeStruct((2, B) + m_sh, f32)],
        compiler_params=pltpu.CompilerParams(
            dimension_semantics=("arbitrary", "arbitrary", "arbitrary"),
            vmem_limit_bytes=V7X_VMEM_BYTES * 3 // 4),
        name="mlstm",
    )(u, u, u, gcol, grow, *state)
    return h[0] + h[1], (c_st, n_st, m_st)


def mlstm_mixer(u_ctx, u_lat, gate_b, norm_g, with_ctx):
    B = u_lat.shape[0]
    f32 = jnp.float32
    zero = (jnp.zeros((2, B, ML_HEADS, ML_HEAD_DIM, ML_HEAD_DIM), f32),
            jnp.zeros((2, B, ML_HEADS, 1, ML_HEAD_DIM), f32),
            jnp.zeros((2, B, ML_HEADS, 1, V7X_LANES), f32))
    h_c, st = bidir_mlstm(u_ctx, gate_b, zero)
    h_l, _ = bidir_mlstm(u_lat, gate_b, st)

    def out(h, u):
        Bq, L = u.shape[:2]
        o = u[..., 3 * ML_WIDTH:4 * ML_WIDTH]
        hn = rms_norm(h.reshape(Bq, L, ML_HEADS, ML_HEAD_DIM), norm_g).reshape(Bq, L, ML_WIDTH)
        return (jax.nn.sigmoid(o.astype(f32)) * hn).astype(u.dtype)

    return (out(h_c, u_ctx) if with_ctx else None), out(h_l, u_lat)


def hyena_filters(L, w1, b1, freq, w2, b2, w3):
    f32 = jnp.float32
    t = jnp.arange(L, dtype=f32) / L
    ang = 2.0 * math.pi * t[:, None] * jnp.arange(1, HY_BANDS + 1, dtype=f32)
    z = jnp.concatenate([t[:, None], jnp.sin(ang), jnp.cos(ang)], axis=-1)
    hdn = jnp.sin(freq[0].astype(f32) * (z @ w1.astype(f32) + b1.astype(f32)))
    hdn = jnp.sin(freq[1].astype(f32) * (hdn @ w2.astype(f32) + b2.astype(f32)))
    filt = (hdn @ w3.astype(f32)).reshape(L, HY_ORDER, 2, HY_WIDTH)
    log_target = math.log(HY_DECAY_TARGET)
    alpha = jnp.linspace(-log_target / HY_LONG_PCT, -log_target / HY_SHORT_PCT, HY_WIDTH, dtype=f32)
    filt = filt * jnp.exp(-t[:, None] * alpha)[:, None, None, :]
    return filt * lax.rsqrt(jnp.sum(filt * filt, axis=(0, 2), keepdims=True) + EPS)


def two_sided_fftconv(z, h_fwd, h_bwd):
    L, C = h_fwd.shape
    h_circ = jnp.concatenate([h_fwd, jnp.zeros((1, C), h_fwd.dtype), h_bwd[:0:-1]], axis=0)
    hf = jnp.fft.rfft(h_circ, n=2 * L, axis=0)
    zf = jnp.fft.rfft(z.astype(jnp.float32), n=2 * L, axis=1)
    y = jnp.fft.irfft(zf * hf[None], n=2 * L, axis=1)[:, :L]
    return y.astype(z.dtype)


def short_conv3(u, w):
    up = jnp.pad(u, ((0, 0), (1, 1), (0, 0)))
    return up[:, :-2] * w[0] + up[:, 1:-1] * w[1] + up[:, 2:] * w[2]


def hyena_mixer(u, conv_w, filt, bias):
    u = short_conv3(u, conv_w)
    v, x1, x2 = jnp.split(u, 3, axis=-1)
    z = v
    for order, gate in enumerate((x1, x2)):
        z = gate * (two_sided_fftconv(z, filt[:, order, 0], filt[:, order, 1]) + z * bias[order])
    return z


def merge_branches(ys, gate_logits, w_branch, w_out):
    B, L = gate_logits.shape[:2]
    gl = gate_logits.reshape(B, L, N_BRANCH, -1)
    acc = jax.nn.sigmoid(gl[:, :, 0]) * (ys[0] @ w_branch[0])
    for i in range(1, N_BRANCH):
        acc = acc + jax.nn.sigmoid(gl[:, :, i]) * (ys[i] @ w_branch[i])
    return acc @ w_out


def _topk_rows(s, k):
    rows = []
    for _ in range(k):
        m = jnp.max(s, axis=0, keepdims=True)
        rows.append(m)
        s = jnp.where(s == m, -jnp.inf, s)
    return rows


def _peer_route_kernel(ht_ref, wqt_ref, sk_ref, s1_ref, s2_ref, ea_ref, eb_ref, thr_ref):
    f32 = jnp.float32
    tn = ht_ref.shape[1]
    half = PEER_DK // 2
    qt = jnp.dot(wqt_ref[...], ht_ref[...], preferred_element_type=f32).astype(sk_ref.dtype)
    s1 = jnp.dot(sk_ref[0, 0], qt[:half], preferred_element_type=f32)
    s2 = jnp.dot(sk_ref[0, 1], qt[half:], preferred_element_type=f32)
    sv1 = _topk_rows(s1, PEER_TOPK)
    sv2 = _topk_rows(s2, PEER_TOPK)
    sv2_stack = jnp.concatenate(sv2, axis=0)
    row = lax.broadcasted_iota(jnp.int32, (8, tn), 0)
    tiles = [sv1[0] + sv2_stack]
    for a in range(1, PEER_TOPK):
        nb = PEER_TOPK // (a + 1)
        t = sv1[a] + sv2_stack[:8]
        if nb < 8:
            t = jnp.where(row < nb, t, -jnp.inf)
        tiles.append(t)
    cand = jnp.concatenate(tiles, axis=0)
    cmax = sv1[0] + sv2[0]
    thr = _topk_rows(cand, PEER_TOPK)[-1]
    z = jnp.sum(jnp.where(cand >= thr, jnp.exp(cand - cmax), 0.0), axis=0, keepdims=True)
    s1_ref[0] = s1
    s2_ref[0] = s2
    ea_ref[0] = jnp.exp(s1 - sv1[0]) * (0.5 / z)
    eb_ref[0] = jnp.exp(s2 - sv2[0])
    thr_ref[0] = thr


GELU_C = 0.7978845608028654


def _twice_gelu_tanh(x):
    return x * (1.0 + jnp.tanh(x * (GELU_C + (GELU_C * 0.044715) * (x * x))))


def _peer_expert_kernel(ht_ref, u_ref, vt_ref, s1_ref, s2_ref, ea_ref, eb_ref, thr_ref, o_ref, a_ref, wa_ref):
    f32 = jnp.float32
    e = pl.program_id(1)
    tn = ht_ref.shape[1]

    @pl.when(e == 0)
    def _init():
        o_ref[...] = jnp.zeros_like(o_ref)

    nsub = PEER_TE // PEER_SUB
    per_sub = PEER_SUB // PEER_NKEYS

    def scores(j):
        a_ref[j % 2] = jnp.dot(u_ref[j * PEER_SUB:(j + 1) * PEER_SUB, :], ht_ref[...], preferred_element_type=f32)

    def combine(j):
        for jc, tc in itertools.product(range(PEER_NKEYS // PEER_JC), range(tn // V7X_LANES)):
            keys = slice(jc * PEER_JC, (jc + 1) * PEER_JC)
            cols = slice(tc * V7X_LANES, (tc + 1) * V7X_LANES)
            w = [jnp.zeros((PEER_JC, V7X_LANES), f32) for _ in range(per_sub)]
            for h in range(PEER_HEADS):
                s2, eb, thr = s2_ref[h, keys, cols], eb_ref[h, keys, cols], thr_ref[h, :, cols]
                for r in range(per_sub):
                    ii = j * per_sub + r
                    s = s2 + s1_ref[h, ii:ii + 1, cols]
                    w[r] = w[r] + jnp.where(s >= thr, eb, 0.0) * ea_ref[h, ii:ii + 1, cols]
            for r in range(per_sub):
                rows = slice(r * PEER_NKEYS + jc * PEER_JC, r * PEER_NKEYS + (jc + 1) * PEER_JC)
                wa_ref[j % 2, rows, cols] = (w[r] * _twice_gelu_tanh(a_ref[j % 2, rows, cols])).astype(wa_ref.dtype)
        o_ref[...] += jnp.dot(vt_ref[:, j * PEER_SUB:(j + 1) * PEER_SUB], wa_ref[j % 2],
                              preferred_element_type=f32)

    scores(0)
    for j in range(nsub):
        if j + 1 < nsub:
            scores(j + 1)
        combine(j)


def peer(h, wq, subkeys, u_tab, v_tab):
    T, D = h.shape
    assert T % PEER_TN == 0
    nt = T // PEER_TN
    ne = PEER_NKEYS // PEER_TI
    f32 = jnp.float32
    ht = h.T.astype(MXU_DTYPE)
    wqt = wq.T.astype(MXU_DTYPE)
    sk = subkeys.astype(MXU_DTYPE)
    u = u_tab.astype(MXU_DTYPE)
    vt = v_tab.T.astype(MXU_DTYPE)
    head_blk = pl.BlockSpec((1, PEER_NKEYS, PEER_TN), lambda n, hh: (hh, 0, n))
    head_shape = jax.ShapeDtypeStruct((PEER_HEADS, PEER_NKEYS, T), f32)
    s1, s2, ea, eb, thr = pl.pallas_call(
        _peer_route_kernel,
        grid=(nt, PEER_HEADS),
        in_specs=[
            pl.BlockSpec((D, PEER_TN), lambda n, hh: (0, n)),
            pl.BlockSpec((PEER_DK, D), lambda n, hh: (hh, 0)),
            pl.BlockSpec((1, 2, PEER_NKEYS, PEER_DK // 2), lambda n, hh: (hh, 0, 0, 0)),
        ],
        out_specs=[head_blk, head_blk, head_blk, head_blk,
                   pl.BlockSpec((1, 1, PEER_TN), lambda n, hh: (hh, 0, n))],
        out_shape=[head_shape, head_shape, head_shape, head_shape,
                   jax.ShapeDtypeStruct((PEER_HEADS, 1, T), f32)],
        compiler_params=pltpu.CompilerParams(
            dimension_semantics=("arbitrary", "arbitrary"),
            vmem_limit_bytes=V7X_VMEM_BYTES * 3 // 4),
        name="peer_route",
    )(ht, wqt, sk)

    tok_blk = pl.BlockSpec((PEER_HEADS, PEER_NKEYS, PEER_TN), lambda n, e: (0, 0, n))
    row_blk = pl.BlockSpec((PEER_HEADS, PEER_TI, PEER_TN), lambda n, e: (0, e, n))
    out_t = pl.pallas_call(
        _peer_expert_kernel,
        grid=(nt, ne),
        in_specs=[
            pl.BlockSpec((D, PEER_TN), lambda n, e: (0, n)),
            pl.BlockSpec((PEER_TE, D), lambda n, e: (e, 0)),
            pl.BlockSpec((D, PEER_TE), lambda n, e: (0, e)),
            row_blk, tok_blk, row_blk, tok_blk,
            pl.BlockSpec((PEER_HEADS, 1, PEER_TN), lambda n, e: (0, 0, n)),
        ],
        out_specs=pl.BlockSpec((D, PEER_TN), lambda n, e: (0, n)),
        out_shape=jax.ShapeDtypeStruct((D, T), f32),
        scratch_shapes=[pltpu.VMEM((2, PEER_SUB, PEER_TN), f32),
                        pltpu.VMEM((2, PEER_SUB, PEER_TN), MXU_DTYPE)],
        compiler_params=pltpu.CompilerParams(
            dimension_semantics=("arbitrary", "arbitrary"),
            vmem_limit_bytes=V7X_VMEM_BYTES * 3 // 4),
        name="peer_expert",
    )(ht, u, vt, s1, s2, ea, eb, thr)
    return out_t.T.astype(h.dtype)


def trunk_layer(xc, xl, mod_c, mod_l, rope_a, rope_b, with_ctx, norm1, norm2, w_in,
                mla_q_norm, mla_kv_norm, mla_w_uq, mla_w_ukv, mla_qk_norm_q, mla_qk_norm_k,
                gqa_qk_norm_q, gqa_qk_norm_k, ml_gate_b, ml_norm,
                hy_conv, hy_w1, hy_b1, hy_freq, hy_w2, hy_b2, hy_w3, hy_bias,
                w_branch, w_out, peer_wq, peer_subkeys, peer_u, peer_v):
    D = xl.shape[-1]
    sh1_c, sc1_c, g1_c, sh2_c, sc2_c, g2_c = jnp.split(mod_c, 6, axis=-1)
    sh1_l, sc1_l, g1_l, sh2_l, sc2_l, g2_l = [a[:, None, :] for a in jnp.split(mod_l, 6, axis=-1)]

    n_mix = IN_OFFSETS[-1]
    hc, hl = modulate(xc, norm1, sh1_c, sc1_c), modulate(xl, norm1, sh1_l, sc1_l)
    a_c, b_c, m_c, h_c = jnp.split(hc @ w_in[:, :n_mix], IN_OFFSETS[:-1], axis=-1)
    a_l, b_l, m_l, h_l = jnp.split(hl @ w_in[:, :n_mix], IN_OFFSETS[:-1], axis=-1)
    gate_c, gate_l = hc @ w_in[:, n_mix:], hl @ w_in[:, n_mix:]

    qa_c, ka_c, va_c = mla_qkv(a_c, mla_q_norm, mla_kv_norm, mla_w_uq, mla_w_ukv, mla_qk_norm_q, mla_qk_norm_k, None)
    qa_l, ka_l, va_l = mla_qkv(a_l, mla_q_norm, mla_kv_norm, mla_w_uq, mla_w_ukv, mla_qk_norm_q, mla_qk_norm_k, rope_a)
    ya_l = latent_attention(qa_l, ka_c, va_c, ka_l, va_l, MLA_QK ** -0.5)
    qb_c, kb_c, vb_c = gqa_qkv(b_c, gqa_qk_norm_q, gqa_qk_norm_k, None)
    qb_l, kb_l, vb_l = gqa_qkv(b_l, gqa_qk_norm_q, gqa_qk_norm_k, rope_b)
    yb_l = latent_attention(qb_l, kb_c, vb_c, kb_l, vb_l, GQA_HEAD_DIM ** -0.5)
    yc_c, yc_l = mlstm_mixer(m_c, m_l, ml_gate_b, ml_norm, with_ctx)
    filt_l = hyena_filters(xl.shape[1], hy_w1, hy_b1, hy_freq, hy_w2, hy_b2, hy_w3)
    yd_l = hyena_mixer(h_l, hy_conv, filt_l, hy_bias)

    xl = xl + g1_l * merge_branches([ya_l, yb_l, yc_l, yd_l], gate_l, w_branch, w_out)
    if with_ctx:
        Bc, Lc = xc.shape[:2]
        ya_c = attend(qa_c, ka_c, va_c, MLA_QK ** -0.5).reshape(Bc, Lc, -1)
        yb_c = attend(qb_c, kb_c, vb_c, GQA_HEAD_DIM ** -0.5).reshape(Bc, Lc, -1)
        filt_c = hyena_filters(Lc, hy_w1, hy_b1, hy_freq, hy_w2, hy_b2, hy_w3)
        yd_c = hyena_mixer(h_c, hy_conv, filt_c, hy_bias)
        xc = xc + g1_c * merge_branches([ya_c, yb_c, yc_c, yd_c], gate_c, w_branch, w_out)

    h2l = modulate(xl, norm2, sh2_l, sc2_l)
    if with_ctx:
        h2c = modulate(xc, norm2, sh2_c, sc2_c)
        n_ctx = xc.shape[0] * xc.shape[1]
        f = peer(jnp.concatenate([h2c.reshape(-1, D), h2l.reshape(-1, D)], axis=0), peer_wq, peer_subkeys, peer_u, peer_v)
        xc = xc + g2_c * f[:n_ctx].reshape(xc.shape)
        xl = xl + g2_l * f[n_ctx:].reshape(xl.shape)
    else:
        xl = xl + g2_l * peer(h2l.reshape(-1, D), peer_wq, peer_subkeys, peer_u, peer_v).reshape(xl.shape)
    return xc, xl


def kernel(x, c, ctx, c_ctx, ada_w, ada_b, norm1, norm2, w_in,
           mla_q_norm, mla_kv_norm, mla_w_uq, mla_w_ukv, mla_qk_norm_q, mla_qk_norm_k,
           gqa_qk_norm_q, gqa_qk_norm_k, ml_gate_b, ml_norm,
           hy_conv, hy_w1, hy_b1, hy_freq, hy_w2, hy_b2, hy_w3, hy_bias,
           w_branch, w_out, peer_wq, peer_subkeys, peer_u, peer_v):
    B, S, D = x.shape
    ROWS = S // GRID_W
    rows = jnp.broadcast_to(jnp.arange(ROWS, dtype=jnp.int32)[:, None], (ROWS, GRID_W)).reshape(-1)
    cols = jnp.broadcast_to(jnp.arange(GRID_W, dtype=jnp.int32)[None, :], (ROWS, GRID_W)).reshape(-1)
    rope_a = axial_rope_tables(rows, cols, MLA_ROPE)
    rope_b = axial_rope_tables(rows, cols, GQA_HEAD_DIM)
    s_lat = jax.nn.silu(c)
    s_ctx = jax.nn.silu(c_ctx)
    xc, xl = ctx, x
    for l in range(DEPTH):
        mod_l = s_lat @ ada_w[l] + ada_b[l]
        mod_c = s_ctx @ ada_w[l] + ada_b[l]
        xc, xl = trunk_layer(
            xc, xl, mod_c, mod_l, rope_a, rope_b, l < DEPTH - 1, norm1[l], norm2[l], w_in[l],
            mla_q_norm[l], mla_kv_norm[l], mla_w_uq[l], mla_w_ukv[l], mla_qk_norm_q[l], mla_qk_norm_k[l],
            gqa_qk_norm_q[l], gqa_qk_norm_k[l], ml_gate_b[l], ml_norm[l],
            hy_conv[l], hy_w1[l], hy_b1[l], hy_freq[l], hy_w2[l], hy_b2[l], hy_w3[l], hy_bias[l],
            w_branch[l], w_out[l], peer_wq[l], peer_subkeys[l], peer_u[l], peer_v[l])
    return xl
```

```python
import functools
import itertools
import math

import jax
import jax.numpy as jnp
from jax import lax
from jax.experimental import pallas as pl
from jax.experimental.pallas import tpu as pltpu

D_MODEL = 1024
DEPTH = 4
GRID_W = 64
ROPE_THETA = 10000.0
EPS = 1e-6

MLA_HEADS = 4
MLA_Q_LORA = 256
MLA_KV_LORA = 128
MLA_NOPE = 64
MLA_ROPE = 32
MLA_V = 64
MLA_QK = MLA_NOPE + MLA_ROPE
MLA_IN = MLA_Q_LORA + MLA_KV_LORA + MLA_ROPE

GQA_Q_HEADS = 4
GQA_KV_HEADS = 2
GQA_GROUP = GQA_Q_HEADS // GQA_KV_HEADS
GQA_HEAD_DIM = 64
GQA_IN = (GQA_Q_HEADS + 2 * GQA_KV_HEADS) * GQA_HEAD_DIM

ML_HEADS = 4
ML_HEAD_DIM = 64
ML_WIDTH = ML_HEADS * ML_HEAD_DIM
ML_CHUNK = 64
ML_IN = 4 * ML_WIDTH + 4 * ML_HEADS

HY_WIDTH = 256
HY_ORDER = 2
HY_BANDS = 16
HY_FEAT = 1 + 2 * HY_BANDS
HY_HIDDEN = 64
HY_IN = (HY_ORDER + 1) * HY_WIDTH
HY_DECAY_TARGET = 1e-2
HY_SHORT_PCT = 0.3
HY_LONG_PCT = 1.5

N_BRANCH = 4
BRANCH_W = 256
GATE_IN = N_BRANCH * D_MODEL
IN_OFFSETS = (MLA_IN, MLA_IN + GQA_IN, MLA_IN + GQA_IN + ML_IN, MLA_IN + GQA_IN + ML_IN + HY_IN)

PEER_HEADS = 8
PEER_NKEYS = 128
PEER_DK = 256
PEER_TOPK = 16
PEER_BLOCK = 128

V7X_LANES = 128
V7X_VMEM_BYTES = 64 * 1024 * 1024

MXU_DTYPE = jnp.bfloat16

PEER_TN = 512
PEER_TI = 8
PEER_TE = PEER_TI * PEER_NKEYS
PEER_SUB = 512
PEER_JC = 32

MERGE_TM = 512
ML_LC = 256

ATT_TQ = 256
ATT_TK = 1280


def _flash_kernel(q_ref, kt_ref, v_ref, o_ref, s_ref, *, tk, nk):
    q = q_ref[0, 0]
    tq = q.shape[0]
    dv = v_ref.shape[-1]

    def scores(j, slot):
        start = pl.multiple_of(j * tk, tk)
        s_ref[slot] = jnp.dot(q, kt_ref[0, 0, :, pl.ds(start, tk)], preferred_element_type=jnp.float32)

    def update(j, slot, m, l, acc):
        s = s_ref[slot]
        m_new = jnp.maximum(m, jnp.max(s, axis=-1, keepdims=True))
        alpha = jnp.exp2(m - m_new)
        p = jnp.exp2(s - m_new)
        l = alpha * l + jnp.sum(p, axis=-1, keepdims=True)
        vv = v_ref[0, 0, pl.ds(pl.multiple_of(j * tk, tk), tk), :]
        acc = alpha * acc + jnp.dot(p.astype(vv.dtype), vv, preferred_element_type=jnp.float32)
        return m_new, l, acc

    def body(i, carry):
        m, l, acc = carry
        scores(2 * i + 1, 1)
        m, l, acc = update(2 * i, 0, m, l, acc)
        scores(2 * i + 2, 0)
        return update(2 * i + 1, 1, m, l, acc)

    m0 = jnp.full((tq, 1), -jnp.inf, jnp.float32)
    l0 = jnp.zeros((tq, 1), jnp.float32)
    acc0 = jnp.zeros((tq, dv), jnp.float32)
    scores(0, 0)
    m, l, acc = lax.fori_loop(0, (nk - 1) // 2, body, (m0, l0, acc0))
    _, l, acc = update(nk - 1, 0, m, l, acc)
    o_ref[0, 0] = (acc / l).astype(o_ref.dtype)


def flash_attention(q, kt, v, group):
    B, H, S, d = q.shape
    K = kt.shape[-1]
    dv = v.shape[-1]
    assert S % ATT_TQ == 0 and K % ATT_TK == 0
    nk = K // ATT_TK
    assert nk % 2 == 1
    return pl.pallas_call(
        functools.partial(_flash_kernel, tk=ATT_TK, nk=nk),
        grid=(B, H, S // ATT_TQ),
        in_specs=[
            pl.BlockSpec((1, 1, ATT_TQ, d), lambda b, h, i: (b, h, i, 0)),
            pl.BlockSpec((1, 1, d, K), lambda b, h, i: (b, h // group, 0, 0)),
            pl.BlockSpec((1, 1, K, dv), lambda b, h, i: (b, h // group, 0, 0)),
        ],
        out_specs=pl.BlockSpec((1, 1, ATT_TQ, dv), lambda b, h, i: (b, h, i, 0)),
        out_shape=jax.ShapeDtypeStruct((B, H, S, dv), jnp.float32),
        scratch_shapes=[pltpu.VMEM((2, ATT_TQ, ATT_TK), jnp.float32)],
        compiler_params=pltpu.CompilerParams(
            dimension_semantics=("arbitrary", "arbitrary", "arbitrary"),
            vmem_limit_bytes=V7X_VMEM_BYTES * 3 // 4),
        name="flash_attention",
    )(q, kt, v)


def latent_attention(q_lat, k_ctx, v_ctx, k_lat, v_lat, scale):
    B, S, Hk, G, dk = q_lat.shape
    k_all = jnp.concatenate([k_ctx, k_lat], axis=1)
    v_all = jnp.concatenate([v_ctx, v_lat], axis=1)
    q = (q_lat * (scale * math.log2(math.e))).astype(jnp.bfloat16).reshape(B, S, Hk * G, dk).transpose(0, 2, 1, 3)
    kt = k_all.astype(jnp.bfloat16).transpose(0, 2, 3, 1)
    v = v_all.astype(jnp.bfloat16).transpose(0, 2, 1, 3)
    o = flash_attention(q, kt, v, G)
    return o.transpose(0, 2, 1, 3).reshape(B, S, -1)


def rms_norm(x, g):
    xf = x.astype(jnp.float32)
    y = xf * lax.rsqrt(jnp.mean(xf * xf, axis=-1, keepdims=True) + EPS)
    return (y * g.astype(jnp.float32)).astype(x.dtype)


def modulate(x, g, shift, scale):
    return rms_norm(x, g) * (1.0 + scale) + shift


def axial_rope_tables(rows, cols, d_rot):
    m = d_rot // 2
    inv = ROPE_THETA ** (-jnp.arange(0, m, 2, dtype=jnp.float32) / m)
    ar = rows.astype(jnp.float32)[:, None] * inv
    ac = cols.astype(jnp.float32)[:, None] * inv
    return (jnp.cos(ar), jnp.sin(ar), jnp.cos(ac), jnp.sin(ac))


def _rotate(x, cos, sin):
    x1, x2 = jnp.split(x, 2, axis=-1)
    return jnp.concatenate([x1 * cos - x2 * sin, x2 * cos + x1 * sin], axis=-1)


def apply_axial_rope(x, tables):
    extra = x.ndim - 3
    t = [a.reshape((a.shape[0],) + (1,) * extra + (a.shape[1],)) for a in tables]
    xr, xc = jnp.split(x.astype(jnp.float32), 2, axis=-1)
    out = jnp.concatenate([_rotate(xr, t[0], t[1]), _rotate(xc, t[2], t[3])], axis=-1)
    return out.astype(x.dtype)


def attend(q, k, v, scale):
    s = jnp.einsum('bqhgd,bkhd->bhgqk', q, k, preferred_element_type=jnp.float32) * scale
    p = jax.nn.softmax(s, axis=-1).astype(v.dtype)
    return jnp.einsum('bhgqk,bkhd->bqhgd', p, v)


def mla_qkv(u, q_norm, kv_norm, w_uq, w_ukv, qk_norm_q, qk_norm_k, rope):
    B, L = u.shape[:2]
    c_q = u[..., :MLA_Q_LORA]
    c_kv = u[..., MLA_Q_LORA:MLA_Q_LORA + MLA_KV_LORA]
    k_pe = u[..., MLA_Q_LORA + MLA_KV_LORA:]
    q = (rms_norm(c_q, q_norm) @ w_uq).reshape(B, L, MLA_HEADS, MLA_QK)
    kv = (rms_norm(c_kv, kv_norm) @ w_ukv).reshape(B, L, MLA_HEADS, MLA_NOPE + MLA_V)
    k = jnp.concatenate([kv[..., :MLA_NOPE],
                         jnp.broadcast_to(k_pe[:, :, None, :], (B, L, MLA_HEADS, MLA_ROPE))], axis=-1)
    v = kv[..., MLA_NOPE:]
    q = rms_norm(q, qk_norm_q)
    k = rms_norm(k, qk_norm_k)
    if rope is not None:
        q = jnp.concatenate([q[..., :MLA_NOPE], apply_axial_rope(q[..., MLA_NOPE:], rope)], axis=-1)
        k = jnp.concatenate([k[..., :MLA_NOPE], apply_axial_rope(k[..., MLA_NOPE:], rope)], axis=-1)
    return q[:, :, :, None, :], k, v


def gqa_qkv(u, qk_norm_q, qk_norm_k, rope):
    B, L = u.shape[:2]
    nq = GQA_Q_HEADS * GQA_HEAD_DIM
    nk = GQA_KV_HEADS * GQA_HEAD_DIM
    q = rms_norm(u[..., :nq].reshape(B, L, GQA_KV_HEADS, GQA_GROUP, GQA_HEAD_DIM), qk_norm_q)
    k = rms_norm(u[..., nq:nq + nk].reshape(B, L, GQA_KV_HEADS, GQA_HEAD_DIM), qk_norm_k)
    v = u[..., nq + nk:].reshape(B, L, GQA_KV_HEADS, GQA_HEAD_DIM)
    if rope is not None:
        q = apply_axial_rope(q, rope)
        k = apply_axial_rope(k, rope)
    return q, k, v


def _mlstm_kernel(q_ref, k_ref, v_ref, gc_ref, gr_ref, c0_ref, n0_ref, m0_ref,
                  h_ref, c_ref, n_ref, m_ref):
    f32 = jnp.float32
    lc = q_ref.shape[1]
    fwd = pl.program_id(0) == 0

    @pl.when(pl.program_id(2) == 0)
    def _init():
        c_ref[...] = c0_ref[...]
        n_ref[...] = n0_ref[...]
        m_ref[...] = m0_ref[...]

    row = lax.broadcasted_iota(jnp.int32, (lc, lc), 0)
    col = lax.broadcasted_iota(jnp.int32, (lc, lc), 1)
    mask = jnp.where(fwd, row, col) >= jnp.where(fwd, col, row)
    tri = mask.astype(f32)
    gc = gc_ref[0, 0]
    gr = gr_ref[0, 0]
    b_cols = jnp.dot(tri, gc[:, :ML_HEADS], preferred_element_type=f32, precision=lax.Precision.HIGHEST)
    b_rows = lax.dot_general(gr[:ML_HEADS], tri, (((1,), (1,)), ((), ())), preferred_element_type=f32,
                             precision=lax.Precision.HIGHEST)
    for hh in range(ML_HEADS):
        sl = slice(hh * ML_HEAD_DIM, (hh + 1) * ML_HEAD_DIM)
        q = q_ref[0, :, sl]
        k = k_ref[0, :, sl] * (ML_HEAD_DIM ** -0.5)
        v = v_ref[0, :, sl]
        qb, vb = q.astype(MXU_DTYPE), v.astype(MXU_DTYPE)
        b_col = b_cols[:, hh:hh + 1]
        b_row = b_rows[hh:hh + 1, :]
        li_col = gc[:, ML_HEADS + hh:ML_HEADS + hh + 1]
        li_row = gr[ML_HEADS + hh:ML_HEADS + hh + 1, :]
        m = m_ref[0, 0, hh][:, :1]
        c_st = c_ref[0, 0, hh]
        n_st = n_ref[0, 0, hh]
        dmat = jnp.where(mask, b_col - b_row + li_row, -jnp.inf)
        inter = b_col + m
        m_t = jnp.maximum(inter, jnp.max(dmat, axis=1, keepdims=True))
        w_intra = jnp.exp(dmat - m_t)
        w_state = jnp.exp(inter - m_t)
        qk = lax.dot_general(qb, k.astype(MXU_DTYPE), (((1,), (1,)), ((), ())), preferred_element_type=f32)
        a = w_intra * qk
        num = (jnp.dot(a.astype(MXU_DTYPE), vb, preferred_element_type=f32)
               + w_state * jnp.dot(qb, c_st.astype(MXU_DTYPE), preferred_element_type=f32))
        den = jnp.sum(a, axis=1, keepdims=True) + w_state * jnp.sum(q * n_st, axis=1, keepdims=True)
        h_ref[0, 0, :, sl] = num / jnp.maximum(jnp.abs(den), jnp.exp(-m_t))
        b_last = jnp.where(fwd, b_col[lc - 1:lc], b_col[0:1])
        g = b_last - b_col + li_col
        m_new = jnp.maximum(b_last + m, jnp.max(g, axis=0, keepdims=True))
        w_c = jnp.exp(b_last + m - m_new)
        kw = k * jnp.exp(g - m_new)
        c_ref[0, 0, hh] = w_c * c_st + lax.dot_general(kw.astype(MXU_DTYPE), vb, (((0,), (0,)), ((), ())),
                                                       preferred_element_type=f32)
        n_ref[0, 0, hh] = w_c * n_st + jnp.sum(kw, axis=0, keepdims=True)
        m_ref[0, 0, hh] = jnp.broadcast_to(m_new, (1, V7X_LANES))


def bidir_mlstm(u, gate_b, state):
    B, L = u.shape[:2]
    f32 = jnp.float32
    lc = min(ML_LC, L)
    assert L % lc == 0
    nc = L // lc
    w = ML_WIDTH
    gates = u[..., 4 * w:].astype(f32).reshape(B, L, 4, ML_HEADS) + gate_b.astype(f32)
    gcol = jnp.stack([jnp.concatenate([jax.nn.log_sigmoid(gates[:, :, 2]), gates[:, :, 0]], axis=-1),
                      jnp.concatenate([jax.nn.log_sigmoid(gates[:, :, 3]), gates[:, :, 1]], axis=-1)])
    grow = gcol.transpose(0, 1, 3, 2)
    chunk = lambda d, c: c + d * (nc - 1 - 2 * c)
    seq_blk = lambda j: pl.BlockSpec((1, lc, w), lambda d, b, c: (b, chunk(d, c), j))
    st_blk = lambda shape: pl.BlockSpec((1, 1) + shape, lambda d, b, c: (d, b) + (0,) * len(shape))
    c_sh, n_sh, m_sh = (ML_HEADS, ML_HEAD_DIM, ML_HEAD_DIM), (ML_HEADS, 1, ML_HEAD_DIM), (ML_HEADS, 1, V7X_LANES)
    h, c_st, n_st, m_st = pl.pallas_call(
        _mlstm_kernel,
        grid=(2, B, nc),
        in_specs=[seq_blk(0), seq_blk(1), seq_blk(2),
                  pl.BlockSpec((1, 1, lc, 2 * ML_HEADS), lambda d, b, c: (d, b, chunk(d, c), 0)),
                  pl.BlockSpec((1, 1, 2 * ML_HEADS, lc), lambda d, b, c: (d, b, 0, chunk(d, c))),
                  st_blk(c_sh), st_blk(n_sh), st_blk(m_sh)],
        out_specs=[pl.BlockSpec((1, 1, lc, w), lambda d, b, c: (d, b, chunk(d, c), 0)),
                   st_blk(c_sh), st_blk(n_sh), st_blk(m_sh)],
        out_shape=[jax.ShapeDtypeStruct((2, B, L, w), f32),
                   jax.ShapeDtypeStruct((2, B) + c_sh, f32),
                   jax.ShapeDtypeStruct((2, B) + n_sh, f32),
                   jax.ShapeDtypeStruct((2, B) + m_sh, f32)],
        compiler_params=pltpu.CompilerParams(
            dimension_semantics=("arbitrary", "arbitrary", "arbitrary"),
            vmem_limit_bytes=V7X_VMEM_BYTES * 3 // 4),
        name="mlstm",
    )(u, u, u, gcol, grow, *state)
    return h[0] + h[1], (c_st, n_st, m_st)


def mlstm_mixer(u_ctx, u_lat, gate_b, norm_g, with_ctx):
    B = u_lat.shape[0]
    f32 = jnp.float32
    zero = (jnp.zeros((2, B, ML_HEADS, ML_HEAD_DIM, ML_HEAD_DIM), f32),
            jnp.zeros((2, B, ML_HEADS, 1, ML_HEAD_DIM), f32),
            jnp.zeros((2, B, ML_HEADS, 1, V7X_LANES), f32))
    h_c, st = bidir_mlstm(u_ctx, gate_b, zero)
    h_l, _ = bidir_mlstm(u_lat, gate_b, st)

    def out(h, u):
        Bq, L = u.shape[:2]
        o = u[..., 3 * ML_WIDTH:4 * ML_WIDTH]
        hn = rms_norm(h.reshape(Bq, L, ML_HEADS, ML_HEAD_DIM), norm_g).reshape(Bq, L, ML_WIDTH)
        return (jax.nn.sigmoid(o.astype(f32)) * hn).astype(u.dtype)

    return (out(h_c, u_ctx) if with_ctx else None), out(h_l, u_lat)


def hyena_filters(L, w1, b1, freq, w2, b2, w3):
    f32 = jnp.float32
    t = jnp.arange(L, dtype=f32) / L
    ang = 2.0 * math.pi * t[:, None] * jnp.arange(1, HY_BANDS + 1, dtype=f32)
    z = jnp.concatenate([t[:, None], jnp.sin(ang), jnp.cos(ang)], axis=-1)
    hdn = jnp.sin(freq[0].astype(f32) * (z @ w1.astype(f32) + b1.astype(f32)))
    hdn = jnp.sin(freq[1].astype(f32) * (hdn @ w2.astype(f32) + b2.astype(f32)))
    filt = (hdn @ w3.astype(f32)).reshape(L, HY_ORDER, 2, HY_WIDTH)
    log_target = math.log(HY_DECAY_TARGET)
    alpha = jnp.linspace(-log_target / HY_LONG_PCT, -log_target / HY_SHORT_PCT, HY_WIDTH, dtype=f32)
    filt = filt * jnp.exp(-t[:, None] * alpha)[:, None, None, :]
    return filt * lax.rsqrt(jnp.sum(filt * filt, axis=(0, 2), keepdims=True) + EPS)


def two_sided_fftconv(z, h_fwd, h_bwd):
    L, C = h_fwd.shape
    h_circ = jnp.concatenate([h_fwd, jnp.zeros((1, C), h_fwd.dtype), h_bwd[:0:-1]], axis=0)
    hf = jnp.fft.rfft(h_circ, n=2 * L, axis=0)
    zf = jnp.fft.rfft(z.astype(jnp.float32), n=2 * L, axis=1)
    y = jnp.fft.irfft(zf * hf[None], n=2 * L, axis=1)[:, :L]
    return y.astype(z.dtype)


def short_conv3(u, w):
    up = jnp.pad(u, ((0, 0), (1, 1), (0, 0)))
    return up[:, :-2] * w[0] + up[:, 1:-1] * w[1] + up[:, 2:] * w[2]


def hyena_mixer(u, conv_w, filt, bias):
    u = short_conv3(u, conv_w)
    v, x1, x2 = jnp.split(u, 3, axis=-1)
    z = v
    for order, gate in enumerate((x1, x2)):
        z = gate * (two_sided_fftconv(z, filt[:, order, 0], filt[:, order, 1]) + z * bias[order])
    return z


def _merge_kernel(ya_ref, yb_ref, yc_ref, yd_ref, gate_ref, wb_ref, wo_ref, x_ref, g1_ref, o_ref):
    f32 = jnp.float32
    d = x_ref.shape[-1]
    acc = jnp.zeros(x_ref.shape[1:], f32)
    for i, y_ref in enumerate((ya_ref, yb_ref, yc_ref, yd_ref)):
        p = jnp.dot(y_ref[0].astype(wb_ref.dtype), wb_ref[i], preferred_element_type=f32)
        acc = acc + jax.nn.sigmoid(gate_ref[0, :, i * d:(i + 1) * d]) * p
    mixed = jnp.dot(acc.astype(wo_ref.dtype), wo_ref[...], preferred_element_type=f32)
    o_ref[0] = x_ref[0] + g1_ref[0] * mixed


def merge_residual(x, ys, gate_logits, w_branch, w_out, g1):
    B, L, D = x.shape
    tm = min(MERGE_TM, L)
    assert L % tm == 0
    tok = lambda width: pl.BlockSpec((1, tm, width), lambda b, n: (b, n, 0))
    return pl.pallas_call(
        _merge_kernel,
        grid=(B, L // tm),
        in_specs=[tok(BRANCH_W), tok(BRANCH_W), tok(BRANCH_W), tok(BRANCH_W), tok(N_BRANCH * D),
                  pl.BlockSpec((N_BRANCH, BRANCH_W, D), lambda b, n: (0, 0, 0)),
                  pl.BlockSpec((D, D), lambda b, n: (0, 0)),
                  tok(D),
                  pl.BlockSpec((1, 1, D), lambda b, n: (b, 0, 0))],
        out_specs=tok(D),
        out_shape=jax.ShapeDtypeStruct((B, L, D), x.dtype),
        compiler_params=pltpu.CompilerParams(
            dimension_semantics=("arbitrary", "arbitrary"),
            vmem_limit_bytes=V7X_VMEM_BYTES * 3 // 4),
        name="merge_residual",
    )(*ys, gate_logits, w_branch.astype(MXU_DTYPE), w_out.astype(MXU_DTYPE), x,
      jnp.broadcast_to(g1, (B, 1, D)))


def _topk_rows(s, k):
    rows = []
    for _ in range(k):
        m = jnp.max(s, axis=0, keepdims=True)
        rows.append(m)
        s = jnp.where(s == m, -jnp.inf, s)
    return rows


def _peer_route_kernel(ht_ref, wqt_ref, sk_ref, s1_ref, s2_ref, ea_ref, eb_ref, thr_ref):
    f32 = jnp.float32
    tn = ht_ref.shape[1]
    half = PEER_DK // 2
    qt = jnp.dot(wqt_ref[...], ht_ref[...], preferred_element_type=f32).astype(sk_ref.dtype)
    s1 = jnp.dot(sk_ref[0, 0], qt[:half], preferred_element_type=f32)
    s2 = jnp.dot(sk_ref[0, 1], qt[half:], preferred_element_type=f32)
    sv1 = _topk_rows(s1, PEER_TOPK)
    sv2 = _topk_rows(s2, PEER_TOPK)
    sv2_stack = jnp.concatenate(sv2, axis=0)
    row = lax.broadcasted_iota(jnp.int32, (8, tn), 0)
    tiles = [sv1[0] + sv2_stack]
    for a in range(1, PEER_TOPK):
        nb = PEER_TOPK // (a + 1)
        t = sv1[a] + sv2_stack[:8]
        if nb < 8:
            t = jnp.where(row < nb, t, -jnp.inf)
        tiles.append(t)
    cand = jnp.concatenate(tiles, axis=0)
    cmax = sv1[0] + sv2[0]
    thr = _topk_rows(cand, PEER_TOPK)[-1]
    z = jnp.sum(jnp.where(cand >= thr, jnp.exp(cand - cmax), 0.0), axis=0, keepdims=True)
    s1_ref[0] = s1
    s2_ref[0] = s2
    ea_ref[0] = jnp.exp(s1 - sv1[0]) * (0.5 / z)
    eb_ref[0] = jnp.exp(s2 - sv2[0])
    thr_ref[0] = thr


GELU_C = 0.7978845608028654


def _twice_gelu_tanh(x):
    return x * (1.0 + jnp.tanh(x * (GELU_C + (GELU_C * 0.044715) * (x * x))))


def _peer_expert_kernel(ht_ref, u_ref, vt_ref, s1_ref, s2_ref, ea_ref, eb_ref, thr_ref, o_ref, a_ref, wa_ref):
    f32 = jnp.float32
    e = pl.program_id(1)
    tn = ht_ref.shape[1]

    @pl.when(e == 0)
    def _init():
        o_ref[...] = jnp.zeros_like(o_ref)

    nsub = PEER_TE // PEER_SUB
    per_sub = PEER_SUB // PEER_NKEYS

    def scores(j):
        a_ref[j % 2] = jnp.dot(u_ref[j * PEER_SUB:(j + 1) * PEER_SUB, :], ht_ref[...], preferred_element_type=f32)

    def combine(j):
        for jc, tc in itertools.product(range(PEER_NKEYS // PEER_JC), range(tn // V7X_LANES)):
            keys = slice(jc * PEER_JC, (jc + 1) * PEER_JC)
            cols = slice(tc * V7X_LANES, (tc + 1) * V7X_LANES)
            w = [jnp.zeros((PEER_JC, V7X_LANES), f32) for _ in range(per_sub)]
            for h in range(PEER_HEADS):
                s2, eb, thr = s2_ref[h, keys, cols], eb_ref[h, keys, cols], thr_ref[h, :, cols]
                for r in range(per_sub):
                    ii = j * per_sub + r
                    s = s2 + s1_ref[h, ii:ii + 1, cols]
                    w[r] = w[r] + jnp.where(s >= thr, eb, 0.0) * ea_ref[h, ii:ii + 1, cols]
            for r in range(per_sub):
                rows = slice(r * PEER_NKEYS + jc * PEER_JC, r * PEER_NKEYS + (jc + 1) * PEER_JC)
                wa_ref[j % 2, rows, cols] = (w[r] * _twice_gelu_tanh(a_ref[j % 2, rows, cols])).astype(wa_ref.dtype)
        o_ref[...] += jnp.dot(vt_ref[:, j * PEER_SUB:(j + 1) * PEER_SUB], wa_ref[j % 2],
                              preferred_element_type=f32)

    scores(0)
    for j in range(nsub):
        if j + 1 < nsub:
            scores(j + 1)
        combine(j)


def peer(h, wq, subkeys, u_tab, v_tab):
    T, D = h.shape
    assert T % PEER_TN == 0
    nt = T // PEER_TN
    ne = PEER_NKEYS // PEER_TI
    f32 = jnp.float32
    ht = h.T.astype(MXU_DTYPE)
    wqt = wq.T.astype(MXU_DTYPE)
    sk = subkeys.astype(MXU_DTYPE)
    u = u_tab.astype(MXU_DTYPE)
    vt = v_tab.T.astype(MXU_DTYPE)
    head_blk = pl.BlockSpec((1, PEER_NKEYS, PEER_TN), lambda n, hh: (hh, 0, n))
    head_shape = jax.ShapeDtypeStruct((PEER_HEADS, PEER_NKEYS, T), f32)
    s1, s2, ea, eb, thr = pl.pallas_call(
        _peer_route_kernel,
        grid=(nt, PEER_HEADS),
        in_specs=[
            pl.BlockSpec((D, PEER_TN), lambda n, hh: (0, n)),
            pl.BlockSpec((PEER_DK, D), lambda n, hh: (hh, 0)),
            pl.BlockSpec((1, 2, PEER_NKEYS, PEER_DK // 2), lambda n, hh: (hh, 0, 0, 0)),
        ],
        out_specs=[head_blk, head_blk, head_blk, head_blk,
                   pl.BlockSpec((1, 1, PEER_TN), lambda n, hh: (hh, 0, n))],
        out_shape=[head_shape, head_shape, head_shape, head_shape,
                   jax.ShapeDtypeStruct((PEER_HEADS, 1, T), f32)],
        compiler_params=pltpu.CompilerParams(
            dimension_semantics=("arbitrary", "arbitrary"),
            vmem_limit_bytes=V7X_VMEM_BYTES * 3 // 4),
        name="peer_route",
    )(ht, wqt, sk)

    tok_blk = pl.BlockSpec((PEER_HEADS, PEER_NKEYS, PEER_TN), lambda n, e: (0, 0, n))
    row_blk = pl.BlockSpec((PEER_HEADS, PEER_TI, PEER_TN), lambda n, e: (0, e, n))
    out_t = pl.pallas_call(
        _peer_expert_kernel,
        grid=(nt, ne),
        in_specs=[
            pl.BlockSpec((D, PEER_TN), lambda n, e: (0, n)),
            pl.BlockSpec((PEER_TE, D), lambda n, e: (e, 0)),
            pl.BlockSpec((D, PEER_TE), lambda n, e: (0, e)),
            row_blk, tok_blk, row_blk, tok_blk,
            pl.BlockSpec((PEER_HEADS, 1, PEER_TN), lambda n, e: (0, 0, n)),
        ],
        out_specs=pl.BlockSpec((D, PEER_TN), lambda n, e: (0, n)),
        out_shape=jax.ShapeDtypeStruct((D, T), f32),
        scratch_shapes=[pltpu.VMEM((2, PEER_SUB, PEER_TN), f32),
                        pltpu.VMEM((2, PEER_SUB, PEER_TN), MXU_DTYPE)],
        compiler_params=pltpu.CompilerParams(
            dimension_semantics=("arbitrary", "arbitrary"),
            vmem_limit_bytes=V7X_VMEM_BYTES * 3 // 4),
        name="peer_expert",
    )(ht, u, vt, s1, s2, ea, eb, thr)
    return out_t.T.astype(h.dtype)


def trunk_layer(xc, xl, mod_c, mod_l, rope_a, rope_b, with_ctx, norm1, norm2, w_in,
                mla_q_norm, mla_kv_norm, mla_w_uq, mla_w_ukv, mla_qk_norm_q, mla_qk_norm_k,
                gqa_qk_norm_q, gqa_qk_norm_k, ml_gate_b, ml_norm,
                hy_conv, hy_w1, hy_b1, hy_freq, hy_w2, hy_b2, hy_w3, hy_bias,
                w_branch, w_out, peer_wq, peer_subkeys, peer_u, peer_v):
    D = xl.shape[-1]
    sh1_c, sc1_c, g1_c, sh2_c, sc2_c, g2_c = jnp.split(mod_c, 6, axis=-1)
    sh1_l, sc1_l, g1_l, sh2_l, sc2_l, g2_l = [a[:, None, :] for a in jnp.split(mod_l, 6, axis=-1)]

    n_mix = IN_OFFSETS[-1]
    hc, hl = modulate(xc, norm1, sh1_c, sc1_c), modulate(xl, norm1, sh1_l, sc1_l)
    a_c, b_c, m_c, h_c = jnp.split(hc @ w_in[:, :n_mix], IN_OFFSETS[:-1], axis=-1)
    a_l, b_l, m_l, h_l = jnp.split(hl @ w_in[:, :n_mix], IN_OFFSETS[:-1], axis=-1)
    gate_c, gate_l = hc @ w_in[:, n_mix:], hl @ w_in[:, n_mix:]

    qa_c, ka_c, va_c = mla_qkv(a_c, mla_q_norm, mla_kv_norm, mla_w_uq, mla_w_ukv, mla_qk_norm_q, mla_qk_norm_k, None)
    qa_l, ka_l, va_l = mla_qkv(a_l, mla_q_norm, mla_kv_norm, mla_w_uq, mla_w_ukv, mla_qk_norm_q, mla_qk_norm_k, rope_a)
    ya_l = latent_attention(qa_l, ka_c, va_c, ka_l, va_l, MLA_QK ** -0.5)
    qb_c, kb_c, vb_c = gqa_qkv(b_c, gqa_qk_norm_q, gqa_qk_norm_k, None)
    qb_l, kb_l, vb_l = gqa_qkv(b_l, gqa_qk_norm_q, gqa_qk_norm_k, rope_b)
    yb_l = latent_attention(qb_l, kb_c, vb_c, kb_l, vb_l, GQA_HEAD_DIM ** -0.5)
    yc_c, yc_l = mlstm_mixer(m_c, m_l, ml_gate_b, ml_norm, with_ctx)
    filt_l = hyena_filters(xl.shape[1], hy_w1, hy_b1, hy_freq, hy_w2, hy_b2, hy_w3)
    yd_l = hyena_mixer(h_l, hy_conv, filt_l, hy_bias)

    xl = merge_residual(xl, [ya_l, yb_l, yc_l, yd_l], gate_l, w_branch, w_out, g1_l)
    if with_ctx:
        Bc, Lc = xc.shape[:2]
        ya_c = attend(qa_c, ka_c, va_c, MLA_QK ** -0.5).reshape(Bc, Lc, -1)
        yb_c = attend(qb_c, kb_c, vb_c, GQA_HEAD_DIM ** -0.5).reshape(Bc, Lc, -1)
        filt_c = hyena_filters(Lc, hy_w1, hy_b1, hy_freq, hy_w2, hy_b2, hy_w3)
        yd_c = hyena_mixer(h_c, hy_conv, filt_c, hy_bias)
        xc = merge_residual(xc, [ya_c, yb_c, yc_c, yd_c], gate_c, w_branch, w_out, g1_c)

    h2l = modulate(xl, norm2, sh2_l, sc2_l)
    if with_ctx:
        h2c = modulate(xc, norm2, sh2_c, sc2_c)
        n_ctx = xc.shape[0] * xc.shape[1]
        f = peer(jnp.concatenate([h2c.reshape(-1, D), h2l.reshape(-1, D)], axis=0), peer_wq, peer_subkeys, peer_u, peer_v)
        xc = xc + g2_c * f[:n_ctx].reshape(xc.shape)
        xl = xl + g2_l * f[n_ctx:].reshape(xl.shape)
    else:
        xl = xl + g2_l * peer(h2l.reshape(-1, D), peer_wq, peer_subkeys, peer_u, peer_v).reshape(xl.shape)
    return xc, xl


def kernel(x, c, ctx, c_ctx, ada_w, ada_b, norm1, norm2, w_in,
           mla_q_norm, mla_kv_norm, mla_w_uq, mla_w_ukv, mla_qk_norm_q, mla_qk_norm_k,
           gqa_qk_norm_q, gqa_qk_norm_k, ml_gate_b, ml_norm,
           hy_conv, hy_w1, hy_b1, hy_freq, hy_w2, hy_b2, hy_w3, hy_bias,
           w_branch, w_out, peer_wq, peer_subkeys, peer_u, peer_v):
    B, S, D = x.shape
    ROWS = S // GRID_W
    rows = jnp.broadcast_to(jnp.arange(ROWS, dtype=jnp.int32)[:, None], (ROWS, GRID_W)).reshape(-1)
    cols = jnp.broadcast_to(jnp.arange(GRID_W, dtype=jnp.int32)[None, :], (ROWS, GRID_W)).reshape(-1)
    rope_a = axial_rope_tables(rows, cols, MLA_ROPE)
    rope_b = axial_rope_tables(rows, cols, GQA_HEAD_DIM)
    s_lat = jax.nn.silu(c)
    s_ctx = jax.nn.silu(c_ctx)
    xc, xl = ctx, x
    for l in range(DEPTH):
        mod_l = s_lat @ ada_w[l] + ada_b[l]
        mod_c = s_ctx @ ada_w[l] + ada_b[l]
        xc, xl = trunk_layer(
            xc, xl, mod_c, mod_l, rope_a, rope_b, l < DEPTH - 1, norm1[l], norm2[l], w_in[l],
            mla_q_norm[l], mla_kv_norm[l], mla_w_uq[l], mla_w_ukv[l], mla_qk_norm_q[l], mla_qk_norm_k[l],
            gqa_qk_norm_q[l], gqa_qk_norm_k[l], ml_gate_b[l], ml_norm[l],
            hy_conv[l], hy_w1[l], hy_b1[l], hy_freq[l], hy_w2[l], hy_b2[l], hy_w3[l], hy_bias[l],
            w_branch[l], w_out[l], peer_wq[l], peer_subkeys[l], peer_u[l], peer_v[l])
    return xl
```

```python
import functools
import itertools
import math

import jax
import jax.numpy as jnp
from jax import lax
from jax.experimental import pallas as pl
from jax.experimental.pallas import tpu as pltpu

D_MODEL = 1024
DEPTH = 4
GRID_W = 64
ROPE_THETA = 10000.0
EPS = 1e-6

MLA_HEADS = 4
MLA_Q_LORA = 256
MLA_KV_LORA = 128
MLA_NOPE = 64
MLA_ROPE = 32
MLA_V = 64
MLA_QK = MLA_NOPE + MLA_ROPE
MLA_IN = MLA_Q_LORA + MLA_KV_LORA + MLA_ROPE

GQA_Q_HEADS = 4
GQA_KV_HEADS = 2
GQA_GROUP = GQA_Q_HEADS // GQA_KV_HEADS
GQA_HEAD_DIM = 64
GQA_IN = (GQA_Q_HEADS + 2 * GQA_KV_HEADS) * GQA_HEAD_DIM

ML_HEADS = 4
ML_HEAD_DIM = 64
ML_WIDTH = ML_HEADS * ML_HEAD_DIM
ML_CHUNK = 64
ML_IN = 4 * ML_WIDTH + 4 * ML_HEADS

HY_WIDTH = 256
HY_ORDER = 2
HY_BANDS = 16
HY_FEAT = 1 + 2 * HY_BANDS
HY_HIDDEN = 64
HY_IN = (HY_ORDER + 1) * HY_WIDTH
HY_DECAY_TARGET = 1e-2
HY_SHORT_PCT = 0.3
HY_LONG_PCT = 1.5

N_BRANCH = 4
BRANCH_W = 256
GATE_IN = N_BRANCH * D_MODEL
IN_OFFSETS = (MLA_IN, MLA_IN + GQA_IN, MLA_IN + GQA_IN + ML_IN, MLA_IN + GQA_IN + ML_IN + HY_IN)

PEER_HEADS = 8
PEER_NKEYS = 128
PEER_DK = 256
PEER_TOPK = 16
PEER_BLOCK = 128

V7X_LANES = 128
V7X_VMEM_BYTES = 64 * 1024 * 1024

MXU_DTYPE = jnp.bfloat16

PEER_TN = 512
PEER_TI = 16
PEER_TE = PEER_TI * PEER_NKEYS
PEER_SUB = 512
PEER_JC = 32

MERGE_TM = 512
ML_LC = 256

ATT_TQ = 256
ATT_TK = 1280


def _flash_kernel(q_ref, kt_ref, v_ref, o_ref, s_ref, *, tk, nk):
    q = q_ref[0, 0]
    tq = q.shape[0]
    dv = v_ref.shape[-1]

    def scores(j, slot):
        start = pl.multiple_of(j * tk, tk)
        s_ref[slot] = jnp.dot(q, kt_ref[0, 0, :, pl.ds(start, tk)], preferred_element_type=jnp.float32)

    def update(j, slot, m, l, acc):
        s = s_ref[slot]
        m_new = jnp.maximum(m, jnp.max(s, axis=-1, keepdims=True))
        alpha = jnp.exp2(m - m_new)
        p = jnp.exp2(s - m_new)
        l = alpha * l + jnp.sum(p, axis=-1, keepdims=True)
        vv = v_ref[0, 0, pl.ds(pl.multiple_of(j * tk, tk), tk), :]
        acc = alpha * acc + jnp.dot(p.astype(vv.dtype), vv, preferred_element_type=jnp.float32)
        return m_new, l, acc

    def body(i, carry):
        m, l, acc = carry
        scores(2 * i + 1, 1)
        m, l, acc = update(2 * i, 0, m, l, acc)
        scores(2 * i + 2, 0)
        return update(2 * i + 1, 1, m, l, acc)

    m0 = jnp.full((tq, 1), -jnp.inf, jnp.float32)
    l0 = jnp.zeros((tq, 1), jnp.float32)
    acc0 = jnp.zeros((tq, dv), jnp.float32)
    scores(0, 0)
    m, l, acc = lax.fori_loop(0, (nk - 1) // 2, body, (m0, l0, acc0))
    _, l, acc = update(nk - 1, 0, m, l, acc)
    o_ref[0, 0] = (acc / l).astype(o_ref.dtype)


def flash_attention(q, kt, v, group):
    B, H, S, d = q.shape
    K = kt.shape[-1]
    dv = v.shape[-1]
    assert S % ATT_TQ == 0 and K % ATT_TK == 0
    nk = K // ATT_TK
    assert nk % 2 == 1
    return pl.pallas_call(
        functools.partial(_flash_kernel, tk=ATT_TK, nk=nk),
        grid=(B, H, S // ATT_TQ),
        in_specs=[
            pl.BlockSpec((1, 1, ATT_TQ, d), lambda b, h, i: (b, h, i, 0)),
            pl.BlockSpec((1, 1, d, K), lambda b, h, i: (b, h // group, 0, 0)),
            pl.BlockSpec((1, 1, K, dv), lambda b, h, i: (b, h // group, 0, 0)),
        ],
        out_specs=pl.BlockSpec((1, 1, ATT_TQ, dv), lambda b, h, i: (b, h, i, 0)),
        out_shape=jax.ShapeDtypeStruct((B, H, S, dv), jnp.float32),
        scratch_shapes=[pltpu.VMEM((2, ATT_TQ, ATT_TK), jnp.float32)],
        compiler_params=pltpu.CompilerParams(
            dimension_semantics=("arbitrary", "arbitrary", "arbitrary"),
            vmem_limit_bytes=V7X_VMEM_BYTES * 3 // 4),
        name="flash_attention",
    )(q, kt, v)


def latent_attention(q_lat, k_ctx, v_ctx, k_lat, v_lat, scale):
    B, S, Hk, G, dk = q_lat.shape
    k_all = jnp.concatenate([k_ctx, k_lat], axis=1)
    v_all = jnp.concatenate([v_ctx, v_lat], axis=1)
    q = (q_lat * (scale * math.log2(math.e))).astype(jnp.bfloat16).reshape(B, S, Hk * G, dk).transpose(0, 2, 1, 3)
    kt = k_all.astype(jnp.bfloat16).transpose(0, 2, 3, 1)
    v = v_all.astype(jnp.bfloat16).transpose(0, 2, 1, 3)
    o = flash_attention(q, kt, v, G)
    return o.transpose(0, 2, 1, 3).reshape(B, S, -1)


def rms_norm(x, g):
    xf = x.astype(jnp.float32)
    y = xf * lax.rsqrt(jnp.mean(xf * xf, axis=-1, keepdims=True) + EPS)
    return (y * g.astype(jnp.float32)).astype(x.dtype)


def modulate(x, g, shift, scale):
    return rms_norm(x, g) * (1.0 + scale) + shift


def axial_rope_tables(rows, cols, d_rot):
    m = d_rot // 2
    inv = ROPE_THETA ** (-jnp.arange(0, m, 2, dtype=jnp.float32) / m)
    ar = rows.astype(jnp.float32)[:, None] * inv
    ac = cols.astype(jnp.float32)[:, None] * inv
    return (jnp.cos(ar), jnp.sin(ar), jnp.cos(ac), jnp.sin(ac))


def _rotate(x, cos, sin):
    x1, x2 = jnp.split(x, 2, axis=-1)
    return jnp.concatenate([x1 * cos - x2 * sin, x2 * cos + x1 * sin], axis=-1)


def apply_axial_rope(x, tables):
    extra = x.ndim - 3
    t = [a.reshape((a.shape[0],) + (1,) * extra + (a.shape[1],)) for a in tables]
    xr, xc = jnp.split(x.astype(jnp.float32), 2, axis=-1)
    out = jnp.concatenate([_rotate(xr, t[0], t[1]), _rotate(xc, t[2], t[3])], axis=-1)
    return out.astype(x.dtype)


def attend(q, k, v, scale):
    s = jnp.einsum('bqhgd,bkhd->bhgqk', q, k, preferred_element_type=jnp.float32) * scale
    p = jax.nn.softmax(s, axis=-1).astype(v.dtype)
    return jnp.einsum('bhgqk,bkhd->bqhgd', p, v)


def mla_qkv(u, q_norm, kv_norm, w_uq, w_ukv, qk_norm_q, qk_norm_k, rope):
    B, L = u.shape[:2]
    c_q = u[..., :MLA_Q_LORA]
    c_kv = u[..., MLA_Q_LORA:MLA_Q_LORA + MLA_KV_LORA]
    k_pe = u[..., MLA_Q_LORA + MLA_KV_LORA:]
    q = (rms_norm(c_q, q_norm) @ w_uq).reshape(B, L, MLA_HEADS, MLA_QK)
    kv = (rms_norm(c_kv, kv_norm) @ w_ukv).reshape(B, L, MLA_HEADS, MLA_NOPE + MLA_V)
    k = jnp.concatenate([kv[..., :MLA_NOPE],
                         jnp.broadcast_to(k_pe[:, :, None, :], (B, L, MLA_HEADS, MLA_ROPE))], axis=-1)
    v = kv[..., MLA_NOPE:]
    q = rms_norm(q, qk_norm_q)
    k = rms_norm(k, qk_norm_k)
    if rope is not None:
        q = jnp.concatenate([q[..., :MLA_NOPE], apply_axial_rope(q[..., MLA_NOPE:], rope)], axis=-1)
        k = jnp.concatenate([k[..., :MLA_NOPE], apply_axial_rope(k[..., MLA_NOPE:], rope)], axis=-1)
    return q[:, :, :, None, :], k, v


def gqa_qkv(u, qk_norm_q, qk_norm_k, rope):
    B, L = u.shape[:2]
    nq = GQA_Q_HEADS * GQA_HEAD_DIM
    nk = GQA_KV_HEADS * GQA_HEAD_DIM
    q = rms_norm(u[..., :nq].reshape(B, L, GQA_KV_HEADS, GQA_GROUP, GQA_HEAD_DIM), qk_norm_q)
    k = rms_norm(u[..., nq:nq + nk].reshape(B, L, GQA_KV_HEADS, GQA_HEAD_DIM), qk_norm_k)
    v = u[..., nq + nk:].reshape(B, L, GQA_KV_HEADS, GQA_HEAD_DIM)
    if rope is not None:
        q = apply_axial_rope(q, rope)
        k = apply_axial_rope(k, rope)
    return q, k, v


def _mlstm_kernel(q_ref, k_ref, v_ref, gc_ref, gr_ref, c0_ref, n0_ref, m0_ref,
                  h_ref, c_ref, n_ref, m_ref):
    f32 = jnp.float32
    lc = q_ref.shape[1]
    fwd = pl.program_id(0) == 0

    @pl.when(pl.program_id(2) == 0)
    def _init():
        c_ref[...] = c0_ref[...]
        n_ref[...] = n0_ref[...]
        m_ref[...] = m0_ref[...]

    row = lax.broadcasted_iota(jnp.int32, (lc, lc), 0)
    col = lax.broadcasted_iota(jnp.int32, (lc, lc), 1)
    mask = jnp.where(fwd, row, col) >= jnp.where(fwd, col, row)
    tri = mask.astype(f32)
    gc = gc_ref[0, 0]
    gr = gr_ref[0, 0]
    b_cols = jnp.dot(tri, gc[:, :ML_HEADS], preferred_element_type=f32, precision=lax.Precision.HIGHEST)
    b_rows = lax.dot_general(gr[:ML_HEADS], tri, (((1,), (1,)), ((), ())), preferred_element_type=f32,
                             precision=lax.Precision.HIGHEST)
    for hh in range(ML_HEADS):
        sl = slice(hh * ML_HEAD_DIM, (hh + 1) * ML_HEAD_DIM)
        q = q_ref[0, :, sl]
        k = k_ref[0, :, sl] * (ML_HEAD_DIM ** -0.5)
        v = v_ref[0, :, sl]
        qb, vb = q.astype(MXU_DTYPE), v.astype(MXU_DTYPE)
        b_col = b_cols[:, hh:hh + 1]
        b_row = b_rows[hh:hh + 1, :]
        li_col = gc[:, ML_HEADS + hh:ML_HEADS + hh + 1]
        li_row = gr[ML_HEADS + hh:ML_HEADS + hh + 1, :]
        m = m_ref[0, 0, hh][:, :1]
        c_st = c_ref[0, 0, hh]
        n_st = n_ref[0, 0, hh]
        dmat = jnp.where(mask, b_col - b_row + li_row, -jnp.inf)
        inter = b_col + m
        m_t = jnp.maximum(inter, jnp.max(dmat, axis=1, keepdims=True))
        w_intra = jnp.exp(dmat - m_t)
        w_state = jnp.exp(inter - m_t)
        qk = lax.dot_general(qb, k.astype(MXU_DTYPE), (((1,), (1,)), ((), ())), preferred_element_type=f32)
        a = w_intra * qk
        num = (jnp.dot(a.astype(MXU_DTYPE), vb, preferred_element_type=f32)
               + w_state * jnp.dot(qb, c_st.astype(MXU_DTYPE), preferred_element_type=f32))
        den = jnp.sum(a, axis=1, keepdims=True) + w_state * jnp.sum(q * n_st, axis=1, keepdims=True)
        h_ref[0, 0, :, sl] = num / jnp.maximum(jnp.abs(den), jnp.exp(-m_t))
        b_last = jnp.where(fwd, b_col[lc - 1:lc], b_col[0:1])
        g = b_last - b_col + li_col
        m_new = jnp.maximum(b_last + m, jnp.max(g, axis=0, keepdims=True))
        w_c = jnp.exp(b_last + m - m_new)
        kw = k * jnp.exp(g - m_new)
        c_ref[0, 0, hh] = w_c * c_st + lax.dot_general(kw.astype(MXU_DTYPE), vb, (((0,), (0,)), ((), ())),
                                                       preferred_element_type=f32)
        n_ref[0, 0, hh] = w_c * n_st + jnp.sum(kw, axis=0, keepdims=True)
        m_ref[0, 0, hh] = jnp.broadcast_to(m_new, (1, V7X_LANES))


def bidir_mlstm(u, gate_b, state):
    B, L = u.shape[:2]
    f32 = jnp.float32
    lc = min(ML_LC, L)
    assert L % lc == 0
    nc = L // lc
    w = ML_WIDTH
    gates = u[..., 4 * w:].astype(f32).reshape(B, L, 4, ML_HEADS) + gate_b.astype(f32)
    gcol = jnp.stack([jnp.concatenate([jax.nn.log_sigmoid(gates[:, :, 2]), gates[:, :, 0]], axis=-1),
                      jnp.concatenate([jax.nn.log_sigmoid(gates[:, :, 3]), gates[:, :, 1]], axis=-1)])
    grow = gcol.transpose(0, 1, 3, 2)
    chunk = lambda d, c: c + d * (nc - 1 - 2 * c)
    seq_blk = lambda j: pl.BlockSpec((1, lc, w), lambda d, b, c: (b, chunk(d, c), j))
    st_blk = lambda shape: pl.BlockSpec((1, 1) + shape, lambda d, b, c: (d, b) + (0,) * len(shape))
    c_sh, n_sh, m_sh = (ML_HEADS, ML_HEAD_DIM, ML_HEAD_DIM), (ML_HEADS, 1, ML_HEAD_DIM), (ML_HEADS, 1, V7X_LANES)
    h, c_st, n_st, m_st = pl.pallas_call(
        _mlstm_kernel,
        grid=(2, B, nc),
        in_specs=[seq_blk(0), seq_blk(1), seq_blk(2),
                  pl.BlockSpec((1, 1, lc, 2 * ML_HEADS), lambda d, b, c: (d, b, chunk(d, c), 0)),
                  pl.BlockSpec((1, 1, 2 * ML_HEADS, lc), lambda d, b, c: (d, b, 0, chunk(d, c))),
                  st_blk(c_sh), st_blk(n_sh), st_blk(m_sh)],
        out_specs=[pl.BlockSpec((1, 1, lc, w), lambda d, b, c: (d, b, chunk(d, c), 0)),
                   st_blk(c_sh), st_blk(n_sh), st_blk(m_sh)],
        out_shape=[jax.ShapeDtypeStruct((2, B, L, w), f32),
                   jax.ShapeDtypeStruct((2, B) + c_sh, f32),
                   jax.ShapeDtypeStruct((2, B) + n_sh, f32),
                   jax.ShapeDtypeStruct((2, B) + m_sh, f32)],
        compiler_params=pltpu.CompilerParams(
            dimension_semantics=("arbitrary", "arbitrary", "arbitrary"),
            vmem_limit_bytes=V7X_VMEM_BYTES * 3 // 4),
        name="mlstm",
    )(u, u, u, gcol, grow, *state)
    return h[0] + h[1], (c_st, n_st, m_st)


def mlstm_mixer(u_ctx, u_lat, gate_b, norm_g, with_ctx):
    B = u_lat.shape[0]
    f32 = jnp.float32
    zero = (jnp.zeros((2, B, ML_HEADS, ML_HEAD_DIM, ML_HEAD_DIM), f32),
            jnp.zeros((2, B, ML_HEADS, 1, ML_HEAD_DIM), f32),
            jnp.zeros((2, B, ML_HEADS, 1, V7X_LANES), f32))
    h_c, st = bidir_mlstm(u_ctx, gate_b, zero)
    h_l, _ = bidir_mlstm(u_lat, gate_b, st)

    def out(h, u):
        Bq, L = u.shape[:2]
        o = u[..., 3 * ML_WIDTH:4 * ML_WIDTH]
        hn = rms_norm(h.reshape(Bq, L, ML_HEADS, ML_HEAD_DIM), norm_g).reshape(Bq, L, ML_WIDTH)
        return (jax.nn.sigmoid(o.astype(f32)) * hn).astype(u.dtype)

    return (out(h_c, u_ctx) if with_ctx else None), out(h_l, u_lat)


def hyena_filters(L, w1, b1, freq, w2, b2, w3):
    f32 = jnp.float32
    t = jnp.arange(L, dtype=f32) / L
    ang = 2.0 * math.pi * t[:, None] * jnp.arange(1, HY_BANDS + 1, dtype=f32)
    z = jnp.concatenate([t[:, None], jnp.sin(ang), jnp.cos(ang)], axis=-1)
    hdn = jnp.sin(freq[0].astype(f32) * (z @ w1.astype(f32) + b1.astype(f32)))
    hdn = jnp.sin(freq[1].astype(f32) * (hdn @ w2.astype(f32) + b2.astype(f32)))
    filt = (hdn @ w3.astype(f32)).reshape(L, HY_ORDER, 2, HY_WIDTH)
    log_target = math.log(HY_DECAY_TARGET)
    alpha = jnp.linspace(-log_target / HY_LONG_PCT, -log_target / HY_SHORT_PCT, HY_WIDTH, dtype=f32)
    filt = filt * jnp.exp(-t[:, None] * alpha)[:, None, None, :]
    return filt * lax.rsqrt(jnp.sum(filt * filt, axis=(0, 2), keepdims=True) + EPS)


def two_sided_fftconv(z, h_fwd, h_bwd):
    L, C = h_fwd.shape
    h_circ = jnp.concatenate([h_fwd, jnp.zeros((1, C), h_fwd.dtype), h_bwd[:0:-1]], axis=0)
    hf = jnp.fft.rfft(h_circ, n=2 * L, axis=0)
    zf = jnp.fft.rfft(z.astype(jnp.float32), n=2 * L, axis=1)
    y = jnp.fft.irfft(zf * hf[None], n=2 * L, axis=1)[:, :L]
    return y.astype(z.dtype)


def short_conv3(u, w):
    up = jnp.pad(u, ((0, 0), (1, 1), (0, 0)))
    return up[:, :-2] * w[0] + up[:, 1:-1] * w[1] + up[:, 2:] * w[2]


def hyena_mixer(u, conv_w, filt, bias):
    u = short_conv3(u, conv_w)
    v, x1, x2 = jnp.split(u, 3, axis=-1)
    z = v
    for order, gate in enumerate((x1, x2)):
        z = gate * (two_sided_fftconv(z, filt[:, order, 0], filt[:, order, 1]) + z * bias[order])
    return z


def _merge_kernel(ya_ref, yb_ref, yc_ref, yd_ref, gate_ref, wb_ref, wo_ref, x_ref, g1_ref, o_ref):
    f32 = jnp.float32
    d = x_ref.shape[-1]
    acc = jnp.zeros(x_ref.shape[1:], f32)
    for i, y_ref in enumerate((ya_ref, yb_ref, yc_ref, yd_ref)):
        p = jnp.dot(y_ref[0].astype(wb_ref.dtype), wb_ref[i], preferred_element_type=f32)
        acc = acc + jax.nn.sigmoid(gate_ref[0, :, i * d:(i + 1) * d]) * p
    mixed = jnp.dot(acc.astype(wo_ref.dtype), wo_ref[...], preferred_element_type=f32)
    o_ref[0] = x_ref[0] + g1_ref[0] * mixed


def merge_residual(x, ys, gate_logits, w_branch, w_out, g1):
    B, L, D = x.shape
    tm = min(MERGE_TM, L)
    assert L % tm == 0
    tok = lambda width: pl.BlockSpec((1, tm, width), lambda b, n: (b, n, 0))
    return pl.pallas_call(
        _merge_kernel,
        grid=(B, L // tm),
        in_specs=[tok(BRANCH_W), tok(BRANCH_W), tok(BRANCH_W), tok(BRANCH_W), tok(N_BRANCH * D),
                  pl.BlockSpec((N_BRANCH, BRANCH_W, D), lambda b, n: (0, 0, 0)),
                  pl.BlockSpec((D, D), lambda b, n: (0, 0)),
                  tok(D),
                  pl.BlockSpec((1, 1, D), lambda b, n: (b, 0, 0))],
        out_specs=tok(D),
        out_shape=jax.ShapeDtypeStruct((B, L, D), x.dtype),
        compiler_params=pltpu.CompilerParams(
            dimension_semantics=("arbitrary", "arbitrary"),
            vmem_limit_bytes=V7X_VMEM_BYTES * 3 // 4),
        name="merge_residual",
    )(*ys, gate_logits, w_branch.astype(MXU_DTYPE), w_out.astype(MXU_DTYPE), x,
      jnp.broadcast_to(g1, (B, 1, D)))


def _topk_rows(s, k):
    rows = []
    for _ in range(k):
        m = jnp.max(s, axis=0, keepdims=True)
        rows.append(m)
        s = jnp.where(s == m, -jnp.inf, s)
    return rows


def _peer_route_kernel(ht_ref, wqt_ref, sk_ref, s1_ref, s2_ref, ea_ref, eb_ref, thr_ref):
    f32 = jnp.float32
    tn = ht_ref.shape[1]
    half = PEER_DK // 2
    qt = jnp.dot(wqt_ref[...], ht_ref[...], preferred_element_type=f32).astype(sk_ref.dtype)
    s1_ref[0] = jnp.dot(sk_ref[0, 0], qt[:half], preferred_element_type=f32)
    s2_ref[0] = jnp.dot(sk_ref[0, 1], qt[half:], preferred_element_type=f32)
    row = lax.broadcasted_iota(jnp.int32, (8, V7X_LANES), 0)
    for tc in range(tn // V7X_LANES):
        cols = slice(tc * V7X_LANES, (tc + 1) * V7X_LANES)
        s1 = s1_ref[0, :, cols]
        s2 = s2_ref[0, :, cols]
        sv1 = _topk_rows(s1, PEER_TOPK)
        sv2 = _topk_rows(s2, PEER_TOPK)
        sv2_stack = jnp.concatenate(sv2, axis=0)
        tiles = [sv1[0] + sv2_stack]
        for a in range(1, PEER_TOPK):
            nb = PEER_TOPK // (a + 1)
            t = sv1[a] + sv2_stack[:8]
            if nb < 8:
                t = jnp.where(row < nb, t, -jnp.inf)
            tiles.append(t)
        cand = jnp.concatenate(tiles, axis=0)
        cmax = sv1[0] + sv2[0]
        thr = _topk_rows(cand, PEER_TOPK)[-1]
        z = jnp.sum(jnp.where(cand >= thr, jnp.exp(cand - cmax), 0.0), axis=0, keepdims=True)
        ea_ref[0, :, cols] = jnp.exp(s1 - sv1[0]) * (0.5 / z)
        eb_ref[0, :, cols] = jnp.exp(s2 - sv2[0])
        thr_ref[0, :, cols] = thr


GELU_C = 0.7978845608028654


def _twice_gelu_tanh(x):
    return x * (1.0 + jnp.tanh(x * (GELU_C + (GELU_C * 0.044715) * (x * x))))


def _peer_expert_kernel(ht_ref, u_ref, vt_ref, s1_ref, s2_ref, ea_ref, eb_ref, thr_ref, o_ref, a_ref, wa_ref):
    f32 = jnp.float32
    e = pl.program_id(1)
    tn = ht_ref.shape[1]

    @pl.when(e == 0)
    def _init():
        o_ref[...] = jnp.zeros_like(o_ref)

    nsub = PEER_TE // PEER_SUB
    per_sub = PEER_SUB // PEER_NKEYS

    def scores(j):
        a_ref[j % 2] = jnp.dot(u_ref[j * PEER_SUB:(j + 1) * PEER_SUB, :], ht_ref[...], preferred_element_type=f32)

    def combine(j):
        for jc, tc in itertools.product(range(PEER_NKEYS // PEER_JC), range(tn // V7X_LANES)):
            keys = slice(jc * PEER_JC, (jc + 1) * PEER_JC)
            cols = slice(tc * V7X_LANES, (tc + 1) * V7X_LANES)
            w = [jnp.zeros((PEER_JC, V7X_LANES), f32) for _ in range(per_sub)]
            for h in range(PEER_HEADS):
                s2, eb, thr = s2_ref[h, keys, cols], eb_ref[h, keys, cols], thr_ref[h, :, cols]
                for r in range(per_sub):
                    ii = j * per_sub + r
                    s = s2 + s1_ref[h, ii:ii + 1, cols]
                    w[r] = w[r] + jnp.where(s >= thr, eb, 0.0) * ea_ref[h, ii:ii + 1, cols]
            for r in range(per_sub):
                rows = slice(r * PEER_NKEYS + jc * PEER_JC, r * PEER_NKEYS + (jc + 1) * PEER_JC)
                wa_ref[j % 2, rows, cols] = (w[r] * _twice_gelu_tanh(a_ref[j % 2, rows, cols])).astype(wa_ref.dtype)
        o_ref[...] += jnp.dot(vt_ref[:, j * PEER_SUB:(j + 1) * PEER_SUB], wa_ref[j % 2],
                              preferred_element_type=f32)

    scores(0)
    for j in range(nsub):
        if j + 1 < nsub:
            scores(j + 1)
        combine(j)


def peer(h, wq, subkeys, u_tab, v_tab):
    T, D = h.shape
    assert T % PEER_TN == 0
    nt = T // PEER_TN
    ne = PEER_NKEYS // PEER_TI
    f32 = jnp.float32
    ht = h.T.astype(MXU_DTYPE)
    wqt = wq.T.astype(MXU_DTYPE)
    sk = subkeys.astype(MXU_DTYPE)
    u = u_tab.astype(MXU_DTYPE)
    vt = v_tab.T.astype(MXU_DTYPE)
    head_blk = pl.BlockSpec((1, PEER_NKEYS, PEER_TN), lambda n, hh: (hh, 0, n))
    head_shape = jax.ShapeDtypeStruct((PEER_HEADS, PEER_NKEYS, T), f32)
    s1, s2, ea, eb, thr = pl.pallas_call(
        _peer_route_kernel,
        grid=(nt, PEER_HEADS),
        in_specs=[
            pl.BlockSpec((D, PEER_TN), lambda n, hh: (0, n)),
            pl.BlockSpec((PEER_DK, D), lambda n, hh: (hh, 0)),
            pl.BlockSpec((1, 2, PEER_NKEYS, PEER_DK // 2), lambda n, hh: (hh, 0, 0, 0)),
        ],
        out_specs=[head_blk, head_blk, head_blk, head_blk,
                   pl.BlockSpec((1, 1, PEER_TN), lambda n, hh: (hh, 0, n))],
        out_shape=[head_shape, head_shape, head_shape, head_shape,
                   jax.ShapeDtypeStruct((PEER_HEADS, 1, T), f32)],
        compiler_params=pltpu.CompilerParams(
            dimension_semantics=("arbitrary", "arbitrary"),
            vmem_limit_bytes=V7X_VMEM_BYTES * 3 // 4),
        name="peer_route",
    )(ht, wqt, sk)

    tok_blk = pl.BlockSpec((PEER_HEADS, PEER_NKEYS, PEER_TN), lambda n, e: (0, 0, n))
    row_blk = pl.BlockSpec((PEER_HEADS, PEER_TI, PEER_TN), lambda n, e: (0, e, n))
    out_t = pl.pallas_call(
        _peer_expert_kernel,
        grid=(nt, ne),
        in_specs=[
            pl.BlockSpec((D, PEER_TN), lambda n, e: (0, n)),
            pl.BlockSpec((PEER_TE, D), lambda n, e: (e, 0)),
            pl.BlockSpec((D, PEER_TE), lambda n, e: (0, e)),
            row_blk, tok_blk, row_blk, tok_blk,
            pl.BlockSpec((PEER_HEADS, 1, PEER_TN), lambda n, e: (0, 0, n)),
        ],
        out_specs=pl.BlockSpec((D, PEER_TN), lambda n, e: (0, n)),
        out_shape=jax.ShapeDtypeStruct((D, T), f32),
        scratch_shapes=[pltpu.VMEM((2, PEER_SUB, PEER_TN), f32),
                        pltpu.VMEM((2, PEER_SUB, PEER_TN), MXU_DTYPE)],
        compiler_params=pltpu.CompilerParams(
            dimension_semantics=("arbitrary", "arbitrary"),
            vmem_limit_bytes=V7X_VMEM_BYTES * 3 // 4),
        name="peer_expert",
    )(ht, u, vt, s1, s2, ea, eb, thr)
    return out_t.T.astype(h.dtype)


def trunk_layer(xc, xl, mod_c, mod_l, rope_a, rope_b, with_ctx, norm1, norm2, w_in,
                mla_q_norm, mla_kv_norm, mla_w_uq, mla_w_ukv, mla_qk_norm_q, mla_qk_norm_k,
                gqa_qk_norm_q, gqa_qk_norm_k, ml_gate_b, ml_norm,
                hy_conv, hy_w1, hy_b1, hy_freq, hy_w2, hy_b2, hy_w3, hy_bias,
                w_branch, w_out, peer_wq, peer_subkeys, peer_u, peer_v):
    D = xl.shape[-1]
    sh1_c, sc1_c, g1_c, sh2_c, sc2_c, g2_c = jnp.split(mod_c, 6, axis=-1)
    sh1_l, sc1_l, g1_l, sh2_l, sc2_l, g2_l = [a[:, None, :] for a in jnp.split(mod_l, 6, axis=-1)]

    n_mix = IN_OFFSETS[-1]
    hc, hl = modulate(xc, norm1, sh1_c, sc1_c), modulate(xl, norm1, sh1_l, sc1_l)
    a_c, b_c, m_c, h_c = jnp.split(hc @ w_in[:, :n_mix], IN_OFFSETS[:-1], axis=-1)
    a_l, b_l, m_l, h_l = jnp.split(hl @ w_in[:, :n_mix], IN_OFFSETS[:-1], axis=-1)
    gate_c, gate_l = hc @ w_in[:, n_mix:], hl @ w_in[:, n_mix:]

    qa_c, ka_c, va_c = mla_qkv(a_c, mla_q_norm, mla_kv_norm, mla_w_uq, mla_w_ukv, mla_qk_norm_q, mla_qk_norm_k, None)
    qa_l, ka_l, va_l = mla_qkv(a_l, mla_q_norm, mla_kv_norm, mla_w_uq, mla_w_ukv, mla_qk_norm_q, mla_qk_norm_k, rope_a)
    ya_l = latent_attention(qa_l, ka_c, va_c, ka_l, va_l, MLA_QK ** -0.5)
    qb_c, kb_c, vb_c = gqa_qkv(b_c, gqa_qk_norm_q, gqa_qk_norm_k, None)
    qb_l, kb_l, vb_l = gqa_qkv(b_l, gqa_qk_norm_q, gqa_qk_norm_k, rope_b)
    yb_l = latent_attention(qb_l, kb_c, vb_c, kb_l, vb_l, GQA_HEAD_DIM ** -0.5)
    yc_c, yc_l = mlstm_mixer(m_c, m_l, ml_gate_b, ml_norm, with_ctx)
    filt_l = hyena_filters(xl.shape[1], hy_w1, hy_b1, hy_freq, hy_w2, hy_b2, hy_w3)
    yd_l = hyena_mixer(h_l, hy_conv, filt_l, hy_bias)

    xl = merge_residual(xl, [ya_l, yb_l, yc_l, yd_l], gate_l, w_branch, w_out, g1_l)
    if with_ctx:
        Bc, Lc = xc.shape[:2]
        ya_c = attend(qa_c, ka_c, va_c, MLA_QK ** -0.5).reshape(Bc, Lc, -1)
        yb_c = attend(qb_c, kb_c, vb_c, GQA_HEAD_DIM ** -0.5).reshape(Bc, Lc, -1)
        filt_c = hyena_filters(Lc, hy_w1, hy_b1, hy_freq, hy_w2, hy_b2, hy_w3)
        yd_c = hyena_mixer(h_c, hy_conv, filt_c, hy_bias)
        xc = merge_residual(xc, [ya_c, yb_c, yc_c, yd_c], gate_c, w_branch, w_out, g1_c)

    h2l = modulate(xl, norm2, sh2_l, sc2_l)
    if with_ctx:
        h2c = modulate(xc, norm2, sh2_c, sc2_c)
        n_ctx = xc.shape[0] * xc.shape[1]
        f = peer(jnp.concatenate([h2c.reshape(-1, D), h2l.reshape(-1, D)], axis=0), peer_wq, peer_subkeys, peer_u, peer_v)
        xc = xc + g2_c * f[:n_ctx].reshape(xc.shape)
        xl = xl + g2_l * f[n_ctx:].reshape(xl.shape)
    else:
        xl = xl + g2_l * peer(h2l.reshape(-1, D), peer_wq, peer_subkeys, peer_u, peer_v).reshape(xl.shape)
    return xc, xl


def kernel(x, c, ctx, c_ctx, ada_w, ada_b, norm1, norm2, w_in,
           mla_q_norm, mla_kv_norm, mla_w_uq, mla_w_ukv, mla_qk_norm_q, mla_qk_norm_k,
           gqa_qk_norm_q, gqa_qk_norm_k, ml_gate_b, ml_norm,
           hy_conv, hy_w1, hy_b1, hy_freq, hy_w2, hy_b2, hy_w3, hy_bias,
           w_branch, w_out, peer_wq, peer_subkeys, peer_u, peer_v):
    B, S, D = x.shape
    ROWS = S // GRID_W
    rows = jnp.broadcast_to(jnp.arange(ROWS, dtype=jnp.int32)[:, None], (ROWS, GRID_W)).reshape(-1)
    cols = jnp.broadcast_to(jnp.arange(GRID_W, dtype=jnp.int32)[None, :], (ROWS, GRID_W)).reshape(-1)
    rope_a = axial_rope_tables(rows, cols, MLA_ROPE)
    rope_b = axial_rope_tables(rows, cols, GQA_HEAD_DIM)
    s_lat = jax.nn.silu(c)
    s_ctx = jax.nn.silu(c_ctx)
    xc, xl = ctx, x
    for l in range(DEPTH):
        mod_l = s_lat @ ada_w[l] + ada_b[l]
        mod_c = s_ctx @ ada_w[l] + ada_b[l]
        xc, xl = trunk_layer(
            xc, xl, mod_c, mod_l, rope_a, rope_b, l < DEPTH - 1, norm1[l], norm2[l], w_in[l],
            mla_q_norm[l], mla_kv_norm[l], mla_w_uq[l], mla_w_ukv[l], mla_qk_norm_q[l], mla_qk_norm_k[l],
            gqa_qk_norm_q[l], gqa_qk_norm_k[l], ml_gate_b[l], ml_norm[l],
            hy_conv[l], hy_w1[l], hy_b1[l], hy_freq[l], hy_w2[l], hy_b2[l], hy_w3[l], hy_bias[l],
            w_branch[l], w_out[l], peer_wq[l], peer_subkeys[l], peer_u[l], peer_v[l])
    return xl
```
